```python
import jax, jax.numpy as jnp
from jax import lax
import numpy as np

D_MODEL = 1024
BATCH = 8
SEQ = 2048
DEPTH = 1

SC_WIDTH = D_MODEL // 2
CONV_W = 3
N_HEADS = 8
HEAD_DIM = D_MODEL // 16
ATTN_WIDTH = N_HEADS * HEAD_DIM
IDX_HEADS = 8
IDX_DIM = 32
TOPK_MAX = 256
QBLOCK = 128
MEM_TOKENS = 256
MEM_HEADS = 4
MEM_HEAD_DIM = D_MODEL // 8
MEM_WIDTH = MEM_HEADS * MEM_HEAD_DIM
N_BRANCH = 3
D_FF = ((8 * D_MODEL // 3 + 127) // 128) * 128
EPS = 1e-6
IN_SPLITS = (SC_WIDTH, SC_WIDTH, SC_WIDTH,
             ATTN_WIDTH, ATTN_WIDTH, ATTN_WIDTH,
             IDX_HEADS * IDX_DIM, IDX_DIM, IDX_HEADS,
             MEM_WIDTH, N_BRANCH * D_MODEL)
IN_WIDTH = sum(IN_SPLITS)

kernel_name = 'hybrid_shortconv_dsa_memxattn_convffn'


def rmsnorm(x, g):
    xf = x.astype(jnp.float32)
    y = xf * lax.rsqrt(jnp.mean(xf * xf, axis=-1, keepdims=True) + EPS)
    return (y * g.astype(jnp.float32)).astype(x.dtype)


def causal_dwconv(u, w):
    return lax.conv_general_dilated(
        u, w[:, None, :].astype(u.dtype), window_strides=(1,),
        padding=[(w.shape[0] - 1, 0)],
        dimension_numbers=('NWC', 'WIO', 'NWC'),
        feature_group_count=u.shape[-1])


def alibi_slopes(n_heads):
    return 2.0 ** (-8.0 * jnp.arange(1, n_heads + 1, dtype=jnp.float32) / n_heads)


def dsa_attention(q, k, v, qi, ki, wi):
    B, T = q.shape[0], q.shape[1]
    nblk = T // QBLOCK
    k_sel = min(TOPK_MAX, T // 4)
    slopes = alibi_slopes(N_HEADS)
    scale = HEAD_DIM ** -0.5
    key_pos = jnp.arange(T, dtype=jnp.int32)
    b_ids = jnp.repeat(jnp.arange(B, dtype=jnp.int32), nblk)
    starts = jnp.tile(jnp.arange(nblk, dtype=jnp.int32) * QBLOCK, B)

    def blocks(a):
        return a.reshape((B * nblk, QBLOCK) + a.shape[2:])

    def one_block(args):
        q_blk, qi_blk, wi_blk, b, start = args
        t = start + jnp.arange(QBLOCK, dtype=jnp.int32)
        idx_logits = jnp.einsum('qhd,sd->qhs', qi_blk, ki[b])
        score = jnp.einsum('qh,qhs->qs', wi_blk, jax.nn.relu(idx_logits)).astype(jnp.float32)
        score = jnp.where(key_pos[None, :] <= t[:, None], score, -jnp.inf)
        _, sel = lax.top_k(score, k_sel)
        valid = sel <= t[:, None]
        k_g = jnp.take(k[b], sel, axis=0)
        v_g = jnp.take(v[b], sel, axis=0)
        logits = jnp.einsum('qhd,qkhd->qhk', q_blk, k_g).astype(jnp.float32) * scale
        dist = (t[:, None] - sel).astype(jnp.float32)
        logits = logits - slopes[None, :, None] * dist[:, None, :]
        logits = jnp.where(valid[:, None, :], logits, -jnp.inf)
        p = jax.nn.softmax(logits, axis=-1).astype(v.dtype)
        return jnp.einsum('qhk,qkhd->qhd', p, v_g)

    out = lax.map(one_block, (blocks(q), blocks(qi), blocks(wi), b_ids, starts))
    return out.reshape(B, T, ATTN_WIDTH)


def hybrid_layer(x, mem, g_mix, w_in, conv_a_w, w_a_out, q_norm_g, k_norm_g, w_b_out,
                 g_mem, w_mem_kv, mq_norm_g, mk_norm_g, w_m_out, w_o,
                 g_ffn, w_up, conv_f_w, w_down):
    B, T, D = x.shape
    M = mem.shape[1]
    xn = rmsnorm(x, g_mix)
    proj = xn @ w_in
    (b_a, c_a, h_a, q, k, v, qi, ki, wi, qm, gate_logits) = jnp.split(
        proj, np.cumsum(IN_SPLITS)[:-1].tolist(), axis=-1)

    y_a = (b_a * causal_dwconv(c_a * h_a, conv_a_w)) @ w_a_out

    q = rmsnorm(q.reshape(B, T, N_HEADS, HEAD_DIM), q_norm_g)
    k = rmsnorm(k.reshape(B, T, N_HEADS, HEAD_DIM), k_norm_g)
    v = v.reshape(B, T, N_HEADS, HEAD_DIM)
    qi = qi.reshape(B, T, IDX_HEADS, IDX_DIM)
    y_b = dsa_attention(q, k, v, qi, ki, wi) @ w_b_out

    memn = rmsnorm(mem, g_mem)
    km, vm = jnp.split(memn @ w_mem_kv, 2, axis=-1)
    qm = rmsnorm(qm.reshape(B, T, MEM_HEADS, MEM_HEAD_DIM), mq_norm_g)
    km = rmsnorm(km.reshape(B, M, MEM_HEADS, MEM_HEAD_DIM), mk_norm_g)
    vm = vm.reshape(B, M, MEM_HEADS, MEM_HEAD_DIM)
    m_logits = jnp.einsum('bthd,bmhd->bhtm', qm, km).astype(jnp.float32) * (MEM_HEAD_DIM ** -0.5)
    m_p = jax.nn.softmax(m_logits, axis=-1).astype(vm.dtype)
    y_m = jnp.einsum('bhtm,bmhd->bthd', m_p, vm).reshape(B, T, MEM_WIDTH) @ w_m_out

    gates = jax.nn.sigmoid(gate_logits.astype(jnp.float32)).astype(x.dtype).reshape(B, T, N_BRANCH, D)
    merged = gates[:, :, 0] * y_a + gates[:, :, 1] * y_b + gates[:, :, 2] * y_m
    x = x + merged @ w_o

    up = causal_dwconv(rmsnorm(x, g_ffn) @ w_up, conv_f_w)
    gate, val = jnp.split(up, 2, axis=-1)
    return x + (jax.nn.silu(gate) * val) @ w_down


def setup_inputs(seed: int = 0) -> dict:
    key = jax.random.key(seed)
    ks = jax.random.split(key, 20)
    f32 = jnp.float32

    def nrm(k, shape, fan_in):
        return jax.random.normal(k, shape, f32) * (fan_in ** -0.5)

    def gain(k, n):
        return 1.0 + 0.02 * jax.random.normal(k, (DEPTH, n), f32)

    return {
        'x': jax.random.normal(ks[0], (BATCH, SEQ, D_MODEL), f32),
        'mem': jax.random.normal(ks[1], (BATCH, MEM_TOKENS, D_MODEL), f32),
        'g_mix': gain(ks[2], D_MODEL),
        'w_in': nrm(ks[3], (DEPTH, D_MODEL, IN_WIDTH), D_MODEL),
        'conv_a_w': nrm(ks[4], (DEPTH, CONV_W, SC_WIDTH), CONV_W),
        'w_a_out': nrm(ks[5], (DEPTH, SC_WIDTH, D_MODEL), SC_WIDTH),
        'q_norm_g': gain(ks[6], HEAD_DIM),
        'k_norm_g': gain(ks[7], HEAD_DIM),
        'w_b_out': nrm(ks[8], (DEPTH, ATTN_WIDTH, D_MODEL), ATTN_WIDTH),
        'g_mem': gain(ks[9], D_MODEL),
        'w_mem_kv': nrm(ks[10], (DEPTH, D_MODEL, 2 * MEM_WIDTH), D_MODEL),
        'mq_norm_g': gain(ks[11], MEM_HEAD_DIM),
        'mk_norm_g': gain(ks[12], MEM_HEAD_DIM),
        'w_m_out': nrm(ks[13], (DEPTH, MEM_WIDTH, D_MODEL), MEM_WIDTH),
        'w_o': nrm(ks[14], (DEPTH, D_MODEL, D_MODEL), D_MODEL),
        'g_ffn': gain(ks[15], D_MODEL),
        'w_up': nrm(ks[16], (DEPTH, D_MODEL, 2 * D_FF), D_MODEL),
        'conv_f_w': nrm(ks[17], (DEPTH, CONV_W, 2 * D_FF), CONV_W),
        'w_down': nrm(ks[18], (DEPTH, D_FF, D_MODEL), D_FF),
    }


def reference(x, mem, g_mix, w_in, conv_a_w, w_a_out, q_norm_g, k_norm_g, w_b_out,
              g_mem, w_mem_kv, mq_norm_g, mk_norm_g, w_m_out, w_o,
              g_ffn, w_up, conv_f_w, w_down):
    for l in range(DEPTH):
        x = hybrid_layer(x, mem, g_mix[l], w_in[l], conv_a_w[l], w_a_out[l],
                         q_norm_g[l], k_norm_g[l], w_b_out[l], g_mem[l], w_mem_kv[l],
                         mq_norm_g[l], mk_norm_g[l], w_m_out[l], w_o[l],
                         g_ffn[l], w_up[l], conv_f_w[l], w_down[l])
    return x
```

```python
import functools

import jax
import jax.numpy as jnp
from jax import lax
from jax.experimental import pallas as pl
from jax.experimental.pallas import tpu as pltpu

F32 = jnp.float32
BF16 = jnp.bfloat16

EPS = 1e-6
N_HEADS = 8
HEAD_DIM = 64
IDX_HEADS = 8
IDX_DIM = 32
MEM_HEADS = 4
MEM_HEAD_DIM = 128
TOPK_MAX = 256
N_BRANCH = 3

LANE = 128
SUBLANE = 8
HEAD_PAD = LANE
POS_SPLIT = 8.0
VMEM_LIMIT = 60 * 1024 * 1024

TM_IN = 512
TM_OUT = 512
QB = 256
KT = 256
FF_CHUNK = 256
BISECT_FIXED = 10
BISECT_ROUND = 4
NEG_BIG = -1e30


def _const_spec(shape):
    nd = len(shape)
    return pl.BlockSpec(shape, lambda *_: (0,) * nd, pipeline_mode=pl.Buffered(1))


def _causal_conv3(u, prev8, w3):
    row = lax.broadcasted_iota(jnp.int32, u.shape, 0)
    p1 = prev8[7:8, :]
    p2 = prev8[6:7, :]
    r1 = jnp.where(row == 0, p1, pltpu.roll(u, 1, axis=0))
    r2 = jnp.where(row == 0, p2, jnp.where(row == 1, p1, pltpu.roll(u, 2, axis=0)))
    return w3[0:1, :] * r2 + w3[1:2, :] * r1 + w3[2:3, :] * u


def _rms_rows(x, g):
    return x * lax.rsqrt(jnp.mean(x * x, axis=-1, keepdims=True) + EPS) * g


def _mem_kv_kernel(mem_ref, gmem_ref, wkv_ref, gmk_ref, kmT_ref, vm_ref):
    memn = _rms_rows(mem_ref[0], gmem_ref[...]).astype(BF16)
    width = MEM_HEADS * MEM_HEAD_DIM
    km = jnp.dot(memn, wkv_ref[:, 0:width], preferred_element_type=F32)
    vm = jnp.dot(memn, wkv_ref[:, width:2 * width], preferred_element_type=F32)
    heads = []
    for h in range(MEM_HEADS):
        kh = km[:, h * MEM_HEAD_DIM:(h + 1) * MEM_HEAD_DIM]
        heads.append(_rms_rows(kh, gmk_ref[...]))
    kmT_ref[0] = jnp.concatenate(heads, axis=1).T.astype(BF16)
    vm_ref[0] = vm.astype(BF16)


def _mem_kv(mem, g_mem, w_kv_bf, mk_g):
    B, M, D = mem.shape
    width = MEM_HEADS * MEM_HEAD_DIM
    return pl.pallas_call(
        _mem_kv_kernel,
        grid=(B,),
        in_specs=[
            pl.BlockSpec((1, M, D), lambda b: (b, 0, 0)),
            _const_spec((1, D)),
            _const_spec((D, 2 * width)),
            _const_spec((1, MEM_HEAD_DIM)),
        ],
        out_specs=[
            pl.BlockSpec((1, width, M), lambda b: (b, 0, 0)),
            pl.BlockSpec((1, M, width), lambda b: (b, 0, 0)),
        ],
        out_shape=[
            jax.ShapeDtypeStruct((B, width, M), BF16),
            jax.ShapeDtypeStruct((B, M, width), BF16),
        ],
        compiler_params=pltpu.CompilerParams(
            dimension_semantics=("arbitrary",), vmem_limit_bytes=VMEM_LIMIT),
        name="mem_kv",
    )(mem, g_mem, w_kv_bf, mk_g)


class _InCols:
    def __init__(self, d_model):
        sc = d_model // 2
        self.sc = sc
        self.bch = 0
        self.q = self.bch + 3 * sc
        self.k = self.q + N_HEADS * HEAD_PAD
        self.v = self.k + N_HEADS * HEAD_PAD
        self.qi = self.v + N_HEADS * HEAD_DIM
        self.kw = self.qi + IDX_HEADS * IDX_DIM
        self.qm = self.kw + LANE
        self.gates = self.qm + MEM_HEADS * MEM_HEAD_DIM
        self.total = self.gates + N_BRANCH * d_model


def _relayout_w_in(w_in, d_model):
    sc = d_model // 2
    att = N_HEADS * HEAD_DIM
    splits = [sc, sc, sc, att, att, att, IDX_HEADS * IDX_DIM, IDX_DIM, IDX_HEADS,
              MEM_HEADS * MEM_HEAD_DIM, N_BRANCH * d_model]
    offs = [0]
    for s in splits:
        offs.append(offs[-1] + s)
    piece = lambda i: w_in[:, offs[i]:offs[i + 1]]

    def pad_heads(w):
        w = w.reshape(d_model, N_HEADS, HEAD_DIM)
        w = jnp.pad(w, ((0, 0), (0, 0), (0, HEAD_PAD - HEAD_DIM)))
        return w.reshape(d_model, N_HEADS * HEAD_PAD)

    kw = jnp.pad(jnp.concatenate([piece(7), piece(8)], axis=1),
                 ((0, 0), (0, LANE - IDX_DIM - IDX_HEADS)))
    cols = [piece(0), piece(1), piece(2), pad_heads(piece(3)), pad_heads(piece(4)), piece(5),
            piece(6), kw, piece(9), piece(10)]
    return jnp.concatenate(cols, axis=1).astype(BF16)


def _in_proj_kernel(cols, tm, d_model,
                    x_ref, gmix_ref, w_ref, convw_ref, waout_ref, gk_ref, gqT_ref,
                    kmT_ref, vm_ref, gmq_ref, wmout_ref,
                    part_ref, g1_ref, qaugT_ref, kaug_ref, vT_ref, qiT_ref, kw_ref, wiT_ref,
                    halo_ref):
    j = pl.program_id(1)
    sc = cols.sc
    xn = _rms_rows(x_ref[0], gmix_ref[...]).astype(BF16)

    def proj(off, width):
        return jnp.dot(xn, w_ref[:, off:off + width], preferred_element_type=F32)

    @pl.when(j == 0)
    def _():
        halo_ref[...] = jnp.zeros_like(halo_ref)

    ch = proj(cols.bch + sc, sc) * proj(cols.bch + 2 * sc, sc)
    conv = _causal_conv3(ch, halo_ref[...], convw_ref[...])
    halo_ref[...] = ch[tm - SUBLANE:tm, :]
    ua = (proj(cols.bch, sc) * conv).astype(BF16)
    ya = jnp.dot(ua, waout_ref[...], preferred_element_type=F32)

    qm = proj(cols.qm, MEM_HEADS * MEM_HEAD_DIM)
    mem_scale = MEM_HEAD_DIM ** -0.5
    heads = []
    for h in range(MEM_HEADS):
        sl = slice(h * MEM_HEAD_DIM, (h + 1) * MEM_HEAD_DIM)
        qh = _rms_rows(qm[:, sl], gmq_ref[...]).astype(BF16)
        lg = jnp.dot(qh, kmT_ref[0, sl, :], preferred_element_type=F32) * mem_scale
        p = jnp.exp(lg - jnp.max(lg, axis=-1, keepdims=True))
        denom = jnp.sum(p, axis=-1, keepdims=True)
        oh = jnp.dot(p.astype(BF16), vm_ref[0, :, sl], preferred_element_type=F32)
        heads.append(oh / denom)
    om = jnp.concatenate(heads, axis=1).astype(BF16)
    ym = jnp.dot(om, wmout_ref[...], preferred_element_type=F32)

    g0 = jax.nn.sigmoid(proj(cols.gates, d_model))
    g2 = jax.nn.sigmoid(proj(cols.gates + 2 * d_model, d_model))
    part_ref[0] = (g0 * ya + g2 * ym).astype(BF16)
    g1_ref[0] = jax.nn.sigmoid(proj(cols.gates + d_model, d_model)).astype(BF16)

    qT = proj(cols.q, N_HEADS * HEAD_PAD).T
    row8 = lax.broadcasted_iota(jnp.int32, (SUBLANE, tm), 0)
    zpad = jnp.zeros((HEAD_PAD - HEAD_DIM - SUBLANE, tm), F32)
    for h in range(N_HEADS):
        blk = qT[h * HEAD_PAD:h * HEAD_PAD + HEAD_DIM, :]
        r = lax.rsqrt(jnp.sum(blk * blk, axis=0, keepdims=True) * (1.0 / HEAD_DIM) + EPS)
        slope = 2.0 ** (-8.0 * (h + 1) / N_HEADS)
        aug = jnp.where(row8 < 2, slope, 0.0).astype(F32)
        full = jnp.concatenate([blk * r * gqT_ref[...], aug, zpad], axis=0)
        qaugT_ref[0, h * HEAD_PAD:(h + 1) * HEAD_PAD, :] = full.astype(BF16)

    kraw = proj(cols.k, N_HEADS * HEAD_PAD)
    posf = (j * tm + lax.broadcasted_iota(jnp.int32, (tm, HEAD_PAD), 0)).astype(F32)
    lane = lax.broadcasted_iota(jnp.int32, (tm, HEAD_PAD), 1)
    pos_hi = jnp.floor(posf * (1.0 / POS_SPLIT)) * POS_SPLIT
    posmat = jnp.where(lane == HEAD_DIM, pos_hi, jnp.where(lane == HEAD_DIM + 1, posf - pos_hi, 0.0))
    for h in range(N_HEADS):
        kg = kraw[:, h * HEAD_PAD:(h + 1) * HEAD_PAD]
        r = lax.rsqrt(jnp.sum(kg * kg, axis=-1, keepdims=True) * (1.0 / HEAD_DIM) + EPS)
        kaug_ref[0, :, h * HEAD_PAD:(h + 1) * HEAD_PAD] = (kg * r * gk_ref[...] + posmat).astype(BF16)

    vT_ref[0] = proj(cols.v, N_HEADS * HEAD_DIM).T.astype(BF16)
    qiT_ref[0] = proj(cols.qi, IDX_HEADS * IDX_DIM).T.astype(BF16)
    kw = proj(cols.kw, LANE)
    kw_ref[0] = kw.astype(BF16)
    wiT_ref[0] = kw.T[IDX_DIM:IDX_DIM + IDX_HEADS, :]


def _in_proj(x, g_mix, w_perm, conv_a_w, w_a_out_bf, gk_pad, gqT, kmT, vm, mq_g, w_m_out_bf, tm):
    B, T, D = x.shape
    cols = _InCols(D)
    M = vm.shape[1]
    memw = MEM_HEADS * MEM_HEAD_DIM
    tok = lambda w: pl.BlockSpec((1, tm, w), lambda b, j: (b, j, 0))
    tokT = lambda w: pl.BlockSpec((1, w, tm), lambda b, j: (b, 0, j))
    return pl.pallas_call(
        functools.partial(_in_proj_kernel, cols, tm, D),
        grid=(B, T // tm),
        in_specs=[
            tok(D),
            _const_spec((1, D)),
            _const_spec((D, cols.total)),
            _const_spec((3, cols.sc)),
            _const_spec((cols.sc, D)),
            _const_spec((1, HEAD_PAD)),
            _const_spec((HEAD_DIM, tm)),
            pl.BlockSpec((1, memw, M), lambda b, j: (b, 0, 0)),
            pl.BlockSpec((1, M, memw), lambda b, j: (b, 0, 0)),
            _const_spec((1, MEM_HEAD_DIM)),
            _const_spec((memw, D)),
        ],
        out_specs=[
            tok(D), tok(D),
            tokT(N_HEADS * HEAD_PAD), tok(N_HEADS * HEAD_PAD),
            tokT(N_HEADS * HEAD_DIM), tokT(IDX_HEADS * IDX_DIM),
            tok(LANE), tokT(IDX_HEADS),
        ],
        out_shape=[
            jax.ShapeDtypeStruct((B, T, D), BF16),
            jax.ShapeDtypeStruct((B, T, D), BF16),
            jax.ShapeDtypeStruct((B, N_HEADS * HEAD_PAD, T), BF16),
            jax.ShapeDtypeStruct((B, T, N_HEADS * HEAD_PAD), BF16),
            jax.ShapeDtypeStruct((B, N_HEADS * HEAD_DIM, T), BF16),
            jax.ShapeDtypeStruct((B, IDX_HEADS * IDX_DIM, T), BF16),
            jax.ShapeDtypeStruct((B, T, LANE), BF16),
            jax.ShapeDtypeStruct((B, IDX_HEADS, T), F32),
        ],
        scratch_shapes=[pltpu.VMEM((SUBLANE, cols.sc), F32)],
        compiler_params=pltpu.CompilerParams(
            dimension_semantics=("arbitrary", "arbitrary"), vmem_limit_bytes=VMEM_LIMIT),
        name="in_proj",
    )(x, g_mix, w_perm, conv_a_w, w_a_out_bf, gk_pad, gqT, kmT, vm, mq_g, w_m_out_bf)


def _dsa_kernel(k_sel, n_keys, kw_ref, qiT_ref, wiT_ref, kaug_ref, qaugT_ref, vT_ref, out_ref,
                s_ref, m_ref, l_ref, acc_ref, jlim_ref):
    i = pl.program_id(1)
    n_tiles = i + 1
    qpos = i * QB + lax.broadcasted_iota(jnp.int32, (1, QB), 1)
    row = lax.broadcasted_iota(jnp.int32, (KT, QB), 0)
    groups = KT // SUBLANE
    inf = jnp.float32(jnp.inf)

    def tile_start(t):
        return pl.multiple_of(t * KT, KT)

    zrows = jnp.zeros((LANE - IDX_DIM, QB), BF16)

    def score_tile(t, carry):
        r0 = tile_start(t)
        kw_t = kw_ref[0, pl.ds(r0, KT), :]
        acc = jnp.zeros((KT, QB), F32)
        for h in range(IDX_HEADS):
            rhs = jnp.concatenate([qiT_ref[0, h * IDX_DIM:(h + 1) * IDX_DIM, :], zrows], axis=0)
            lg = jnp.dot(kw_t, rhs, preferred_element_type=F32)
            acc = acc + wiT_ref[0, h:h + 1, :] * jnp.maximum(lg, 0.0)
        s_ref[pl.ds(r0, KT), :] = jnp.where(r0 + row <= qpos, acc, -inf)
        return carry

    lax.fori_loop(0, n_tiles, score_tile, 0)

    def column_pass(tile_fn, combine, init):
        def body(t, part):
            r0 = tile_start(t)
            s3 = s_ref[pl.ds(r0, KT), :].reshape(groups, SUBLANE, QB)
            return combine(part, tile_fn(s3, r0))
        part = lax.fori_loop(0, n_tiles, body, jnp.full((SUBLANE, QB), init, F32))
        return part

    def bcast8(v):
        return jnp.broadcast_to(v, (SUBLANE, QB))[None]

    def count_where(indicator_fn):
        part = column_pass(lambda s3, r0: jnp.sum(indicator_fn(s3, r0), axis=0),
                           lambda a, b: a + b, 0.0)
        return jnp.sum(part, axis=0, keepdims=True)

    def count_ge(v):
        v8 = bcast8(v)
        return count_where(lambda s3, r0: jnp.where(s3 >= v8, 1.0, 0.0))

    def count_gt(v):
        v8 = bcast8(v)
        return count_where(lambda s3, r0: jnp.where(s3 > v8, 1.0, 0.0))

    def min_ge(v):
        v8 = bcast8(v)
        part = column_pass(lambda s3, r0: jnp.min(jnp.where(s3 >= v8, s3, inf), axis=0),
                           jnp.minimum, jnp.inf)
        return jnp.min(part, axis=0, keepdims=True)

    kf = jnp.minimum(qpos + 1, k_sel).astype(F32)
    hi0 = jnp.max(column_pass(lambda s3, r0: jnp.max(s3, axis=0), jnp.maximum, -jnp.inf),
                  axis=0, keepdims=True)
    lo0 = min_ge(jnp.full((1, QB), NEG_BIG, F32))
    clo0 = (qpos + 1).astype(F32)

    def bisect(_, st):
        lo, hi, clo = st
        mid = lo + (hi - lo) * 0.5
        mid = jnp.where(mid <= lo, hi, mid)
        c = count_ge(mid)
        ge = c >= kf
        return jnp.where(ge, mid, lo), jnp.where(ge, hi, mid), jnp.where(ge, c, clo)

    def snap(lo):
        tau = min_ge(lo)
        return tau, count_gt(tau)

    def pending(clo, cgt):
        done = jnp.logical_or(clo == kf, cgt < kf)
        return jnp.max(jnp.where(done, 0.0, 1.0))

    lo, hi, clo = lax.fori_loop(0, BISECT_FIXED, bisect, (lo0, hi0, clo0))
    tau, cgt = snap(lo)

    def refine_cond(st):
        return st[5] > 0.0

    def refine(st):
        lo, hi, clo = lax.fori_loop(0, BISECT_ROUND, bisect, st[:3])
        tau, cgt = snap(lo)
        return lo, hi, clo, tau, cgt, pending(clo, cgt)

    lo, hi, clo, tau, cgt, _ = lax.while_loop(
        refine_cond, refine, (lo, hi, clo, tau, cgt, pending(clo, cgt)))

    need = kf - cgt
    excess = clo > kf
    jlim_ref[...] = jnp.full((1, QB), float(n_keys), F32)

    @pl.when(jnp.max(jnp.where(excess, 1.0, 0.0)) > 0.0)
    def _():
        tau8 = bcast8(tau)
        pos3 = lax.broadcasted_iota(jnp.int32, (groups, SUBLANE, QB), 0) * SUBLANE + \
            lax.broadcasted_iota(jnp.int32, (groups, SUBLANE, QB), 1)

        def step(_, st):
            jlo, jhi = st
            jmid = jnp.floor((jlo + jhi) * 0.5)
            j8 = bcast8(jmid)
            f = count_where(lambda s3, r0: jnp.where(
                s3 == tau8, jnp.where((r0 + pos3).astype(F32) <= j8, 1.0, 0.0), 0.0))
            ok = f >= need
            return jnp.where(ok, jlo, jmid), jnp.where(ok, jmid, jhi)

        _, jhi = lax.fori_loop(
            0, n_keys.bit_length(), step,
            (jnp.full((1, QB), -1.0, F32), jnp.full((1, QB), float(n_keys - 1), F32)))
        jlim_ref[...] = jnp.where(excess, jhi, jlim_ref[...])

    jlim = jlim_ref[...]

    m_ref[...] = jnp.full(m_ref.shape, NEG_BIG, F32)
    l_ref[...] = jnp.zeros(l_ref.shape, F32)
    acc_ref[...] = jnp.zeros(acc_ref.shape, F32)

    def attn_tile(t, carry):
        r0 = tile_start(t)
        s_t = s_ref[pl.ds(r0, KT), :]
        kposf = (r0 + row).astype(F32)
        bias = jnp.where(s_t > tau, 0.0,
                         jnp.where(s_t == tau, jnp.where(kposf <= jlim, 0.0, -inf), -inf))
        for h in range(N_HEADS):
            hs = slice(h * HEAD_PAD, (h + 1) * HEAD_PAD)
            vs = slice(h * HEAD_DIM, (h + 1) * HEAD_DIM)
            s = jnp.dot(kaug_ref[0, pl.ds(r0, KT), hs], qaugT_ref[0, hs, :],
                        preferred_element_type=F32) + bias
            m_old = m_ref[h:h + 1, :]
            m_new = jnp.maximum(m_old, jnp.max(s, axis=0, keepdims=True))
            alpha = jnp.exp(m_old - m_new)
            p = jnp.exp(s - m_new)
            l_ref[h:h + 1, :] = alpha * l_ref[h:h + 1, :] + jnp.sum(p, axis=0, keepdims=True)
            pv = jnp.dot(vT_ref[0, vs, pl.ds(r0, KT)], p.astype(BF16),
                         preferred_element_type=F32)
            acc_ref[vs, :] = alpha * acc_ref[vs, :] + pv
            m_ref[h:h + 1, :] = m_new
        return carry

    lax.fori_loop(0, n_tiles, attn_tile, 0)

    for h in range(N_HEADS):
        vs = slice(h * HEAD_DIM, (h + 1) * HEAD_DIM)
        out_ref[0, vs, :] = (acc_ref[vs, :] / l_ref[h:h + 1, :]).astype(BF16)


def _dsa(kw, qiT, wiT, kaug, qaugT, vT, k_sel):
    B, T, _ = kaug.shape
    assert T % QB == 0 and QB == KT
    att = N_HEADS * HEAD_DIM
    qblk = lambda w: pl.BlockSpec((1, w, QB), lambda b, i: (b, 0, i))
    return pl.pallas_call(
        functools.partial(_dsa_kernel, k_sel, T),
        grid=(B, T // QB),
        in_specs=[
            pl.BlockSpec((1, T, LANE), lambda b, i: (b, 0, 0)),
            qblk(IDX_HEADS * IDX_DIM),
            qblk(IDX_HEADS),
            pl.BlockSpec((1, T, N_HEADS * HEAD_PAD), lambda b, i: (b, 0, 0)),
            qblk(N_HEADS * HEAD_PAD),
            pl.BlockSpec((1, att, T), lambda b, i: (b, 0, 0)),
        ],
        out_specs=qblk(att),
        out_shape=jax.ShapeDtypeStruct((B, att, T), BF16),
        scratch_shapes=[
            pltpu.VMEM((T, QB), F32),
            pltpu.VMEM((N_HEADS, QB), F32),
            pltpu.VMEM((N_HEADS, QB), F32),
            pltpu.VMEM((att, QB), F32),
            pltpu.VMEM((1, QB), F32),
        ],
        compiler_params=pltpu.CompilerParams(
            dimension_semantics=("arbitrary", "arbitrary"), vmem_limit_bytes=VMEM_LIMIT),
        name="dsa",
    )(kw, qiT, wiT, kaug, qaugT, vT)


def _out_ffn_kernel(tm, d_ff, x_ref, part_ref, g1_ref, attnT_ref, wb_ref, wo_ref, gffn_ref,
                    wup_ref, convf_ref, wdown_ref, out_ref, halo_ref):
    j = pl.program_id(1)
    yb = lax.dot_general(attnT_ref[0], wb_ref[...], (((0,), (0,)), ((), ())),
                         preferred_element_type=F32)
    merged = (part_ref[0].astype(F32) + g1_ref[0].astype(F32) * yb).astype(BF16)
    x1 = x_ref[0] + jnp.dot(merged, wo_ref[...], preferred_element_type=F32)
    xn2 = _rms_rows(x1, gffn_ref[...]).astype(BF16)

    @pl.when(j == 0)
    def _():
        halo_ref[...] = jnp.zeros_like(halo_ref)

    def conv_up(off):
        cs = slice(off, off + FF_CHUNK)
        up = jnp.dot(xn2, wup_ref[:, cs], preferred_element_type=F32)
        out = _causal_conv3(up, halo_ref[:, cs], convf_ref[:, cs])
        halo_ref[:, cs] = up[tm - SUBLANE:tm, :]
        return out

    acc = x1
    for c in range(d_ff // FF_CHUNK):
        gate = conv_up(c * FF_CHUNK)
        val = conv_up(d_ff + c * FF_CHUNK)
        hmid = (gate * jax.nn.sigmoid(gate) * val).astype(BF16)
        acc = acc + jnp.dot(hmid, wdown_ref[c * FF_CHUNK:(c + 1) * FF_CHUNK, :],
                            preferred_element_type=F32)
    out_ref[0] = acc


def _out_ffn(x, part, g1, attnT, w_b_bf, w_o_bf, g_ffn, w_up_bf, conv_f_w, w_down_bf, tm):
    B, T, D = x.shape
    d_ff = w_down_bf.shape[0]
    att = N_HEADS * HEAD_DIM
    assert d_ff % FF_CHUNK == 0
    tok = lambda w: pl.BlockSpec((1, tm, w), lambda b, j: (b, j, 0))
    return pl.pallas_call(
        functools.partial(_out_ffn_kernel, tm, d_ff),
        grid=(B, T // tm),
        in_specs=[
            tok(D), tok(D), tok(D),
            pl.BlockSpec((1, att, tm), lambda b, j: (b, 0, j)),
            _const_spec((att, D)),
            _const_spec((D, D)),
            _const_spec((1, D)),
            _const_spec((D, 2 * d_ff)),
            _const_spec((3, 2 * d_ff)),
            _const_spec((d_ff, D)),
        ],
        out_specs=tok(D),
        out_shape=jax.ShapeDtypeStruct((B, T, D), F32),
        scratch_shapes=[pltpu.VMEM((SUBLANE, 2 * d_ff), F32)],
        compiler_params=pltpu.CompilerParams(
            dimension_semantics=("arbitrary", "arbitrary"), vmem_limit_bytes=VMEM_LIMIT),
        name="out_ffn",
    )(x, part, g1, attnT, w_b_bf, w_o_bf, g_ffn, w_up_bf, conv_f_w, w_down_bf)


def _layer(x, mem, g_mix, w_in, conv_a_w, w_a_out, q_norm_g, k_norm_g, w_b_out, g_mem, w_mem_kv,
           mq_norm_g, mk_norm_g, w_m_out, w_o, g_ffn, w_up, conv_f_w, w_down):
    B, T, D = x.shape
    tm_in = min(TM_IN, T)
    tm_out = min(TM_OUT, T)
    k_sel = min(TOPK_MAX, T // 4)
    row = lambda g: g.reshape(1, -1).astype(F32)

    kmT, vm = _mem_kv(mem, row(g_mem), w_mem_kv.astype(BF16), row(mk_norm_g))

    scale = HEAD_DIM ** -0.5
    gk_pad = jnp.pad(row(k_norm_g), ((0, 0), (0, HEAD_PAD - HEAD_DIM)))
    gqT = jnp.broadcast_to((q_norm_g.astype(F32) * scale)[:, None], (HEAD_DIM, tm_in))
    part, g1, qaugT, kaug, vT, qiT, kw, wiT = _in_proj(
        x, row(g_mix), _relayout_w_in(w_in, D), conv_a_w, w_a_out.astype(BF16), gk_pad, gqT,
        kmT, vm, row(mq_norm_g), w_m_out.astype(BF16), tm_in)

    attnT = _dsa(kw, qiT, wiT, kaug, qaugT, vT, k_sel)

    return _out_ffn(x, part, g1, attnT, w_b_out.astype(BF16), w_o.astype(BF16), row(g_ffn),
                    w_up.astype(BF16), conv_f_w, w_down.astype(BF16), tm_out)


def kernel(x, mem, g_mix, w_in, conv_a_w, w_a_out, q_norm_g, k_norm_g, w_b_out, g_mem, w_mem_kv,
           mq_norm_g, mk_norm_g, w_m_out, w_o, g_ffn, w_up, conv_f_w, w_down):
    for l in range(g_mix.shape[0]):
        x = _layer(x, mem, g_mix[l], w_in[l], conv_a_w[l], w_a_out[l], q_norm_g[l], k_norm_g[l],
                   w_b_out[l], g_mem[l], w_mem_kv[l], mq_norm_g[l], mk_norm_g[l], w_m_out[l],
                   w_o[l], g_ffn[l], w_up[l], conv_f_w[l], w_down[l])
    return x
```

```python
import functools

import jax
import jax.numpy as jnp
from jax import lax
from jax.experimental import pallas as pl
from jax.experimental.pallas import tpu as pltpu

F32 = jnp.float32
BF16 = jnp.bfloat16

EPS = 1e-6
N_HEADS = 8
HEAD_DIM = 64
IDX_HEADS = 8
IDX_DIM = 32
MEM_HEADS = 4
MEM_HEAD_DIM = 128
TOPK_MAX = 256
N_BRANCH = 3

LANE = 128
SUBLANE = 8
HEAD_PAD = LANE
POS_SPLIT = 8.0
VMEM_LIMIT = 60 * 1024 * 1024

TM_IN = 512
TM_OUT = 512
QB = 256
KT = 256
FF_CHUNK = 256
BISECT_FIXED = 10
BISECT_ROUND = 4
NEG_BIG = -1e30


def _const_spec(shape):
    nd = len(shape)
    return pl.BlockSpec(shape, lambda *_: (0,) * nd, pipeline_mode=pl.Buffered(1))


def _causal_conv3(u, prev8, w3):
    row = lax.broadcasted_iota(jnp.int32, u.shape, 0)
    p1 = prev8[7:8, :]
    p2 = prev8[6:7, :]
    r1 = jnp.where(row == 0, p1, pltpu.roll(u, 1, axis=0))
    r2 = jnp.where(row == 0, p2, jnp.where(row == 1, p1, pltpu.roll(u, 2, axis=0)))
    return w3[0:1, :] * r2 + w3[1:2, :] * r1 + w3[2:3, :] * u


def _rms_rows(x, g):
    return x * lax.rsqrt(jnp.mean(x * x, axis=-1, keepdims=True) + EPS) * g


def _mem_kv_kernel(mem_ref, gmem_ref, wkv_ref, gmk_ref, kmT_ref, vm_ref):
    memn = _rms_rows(mem_ref[0], gmem_ref[...]).astype(BF16)
    width = MEM_HEADS * MEM_HEAD_DIM
    km = jnp.dot(memn, wkv_ref[:, 0:width], preferred_element_type=F32)
    vm = jnp.dot(memn, wkv_ref[:, width:2 * width], preferred_element_type=F32)
    heads = []
    for h in range(MEM_HEADS):
        kh = km[:, h * MEM_HEAD_DIM:(h + 1) * MEM_HEAD_DIM]
        heads.append(_rms_rows(kh, gmk_ref[...]))
    kmT_ref[0] = jnp.concatenate(heads, axis=1).T.astype(BF16)
    vm_ref[0] = vm.astype(BF16)


def _mem_kv(mem, g_mem, w_kv_bf, mk_g):
    B, M, D = mem.shape
    width = MEM_HEADS * MEM_HEAD_DIM
    return pl.pallas_call(
        _mem_kv_kernel,
        grid=(B,),
        in_specs=[
            pl.BlockSpec((1, M, D), lambda b: (b, 0, 0)),
            _const_spec((1, D)),
            _const_spec((D, 2 * width)),
            _const_spec((1, MEM_HEAD_DIM)),
        ],
        out_specs=[
            pl.BlockSpec((1, width, M), lambda b: (b, 0, 0)),
            pl.BlockSpec((1, M, width), lambda b: (b, 0, 0)),
        ],
        out_shape=[
            jax.ShapeDtypeStruct((B, width, M), BF16),
            jax.ShapeDtypeStruct((B, M, width), BF16),
        ],
        compiler_params=pltpu.CompilerParams(
            dimension_semantics=("arbitrary",), vmem_limit_bytes=VMEM_LIMIT),
        name="mem_kv",
    )(mem, g_mem, w_kv_bf, mk_g)


class _InCols:
    def __init__(self, d_model):
        sc = d_model // 2
        self.sc = sc
        self.bch = 0
        self.q = self.bch + 3 * sc
        self.k = self.q + N_HEADS * HEAD_PAD
        self.v = self.k + N_HEADS * HEAD_PAD
        self.qi = self.v + N_HEADS * HEAD_DIM
        self.kw = self.qi + IDX_HEADS * IDX_DIM
        self.qm = self.kw + LANE
        self.gates = self.qm + MEM_HEADS * MEM_HEAD_DIM
        self.total = self.gates + N_BRANCH * d_model


def _relayout_w_in(w_in, d_model):
    sc = d_model // 2
    att = N_HEADS * HEAD_DIM
    splits = [sc, sc, sc, att, att, att, IDX_HEADS * IDX_DIM, IDX_DIM, IDX_HEADS,
              MEM_HEADS * MEM_HEAD_DIM, N_BRANCH * d_model]
    offs = [0]
    for s in splits:
        offs.append(offs[-1] + s)
    piece = lambda i: w_in[:, offs[i]:offs[i + 1]]

    def pad_heads(w):
        w = w.reshape(d_model, N_HEADS, HEAD_DIM)
        w = jnp.pad(w, ((0, 0), (0, 0), (0, HEAD_PAD - HEAD_DIM)))
        return w.reshape(d_model, N_HEADS * HEAD_PAD)

    kw = jnp.pad(jnp.concatenate([piece(7), piece(8)], axis=1),
                 ((0, 0), (0, LANE - IDX_DIM - IDX_HEADS)))
    cols = [piece(0), piece(1), piece(2), pad_heads(piece(3)), pad_heads(piece(4)), piece(5),
            piece(6), kw, piece(9), piece(10)]
    return jnp.concatenate(cols, axis=1).astype(BF16)


def _in_proj_kernel(cols, tm, d_model,
                    x_ref, gmix_ref, w_ref, convw_ref, waout_ref, gk_ref, gqT_ref,
                    kmT_ref, vm_ref, gmq_ref, wmout_ref,
                    part_ref, g1_ref, qaugT_ref, kaug_ref, vT_ref, qiT_ref, kw_ref, wiT_ref,
                    halo_ref):
    j = pl.program_id(1)
    sc = cols.sc
    xn = _rms_rows(x_ref[0], gmix_ref[...]).astype(BF16)

    def proj(off, width):
        return jnp.dot(xn, w_ref[:, off:off + width], preferred_element_type=F32)

    @pl.when(j == 0)
    def _():
        halo_ref[...] = jnp.zeros_like(halo_ref)

    ch = proj(cols.bch + sc, sc) * proj(cols.bch + 2 * sc, sc)
    conv = _causal_conv3(ch, halo_ref[...], convw_ref[...])
    halo_ref[...] = ch[tm - SUBLANE:tm, :]
    ua = (proj(cols.bch, sc) * conv).astype(BF16)
    ya = jnp.dot(ua, waout_ref[...], preferred_element_type=F32)

    qm = proj(cols.qm, MEM_HEADS * MEM_HEAD_DIM)
    mem_scale = MEM_HEAD_DIM ** -0.5
    heads = []
    for h in range(MEM_HEADS):
        sl = slice(h * MEM_HEAD_DIM, (h + 1) * MEM_HEAD_DIM)
        qh = _rms_rows(qm[:, sl], gmq_ref[...]).astype(BF16)
        lg = jnp.dot(qh, kmT_ref[0, sl, :], preferred_element_type=F32) * mem_scale
        p = jnp.exp(lg - jnp.max(lg, axis=-1, keepdims=True))
        denom = jnp.sum(p, axis=-1, keepdims=True)
        oh = jnp.dot(p.astype(BF16), vm_ref[0, :, sl], preferred_element_type=F32)
        heads.append(oh / denom)
    om = jnp.concatenate(heads, axis=1).astype(BF16)
    ym = jnp.dot(om, wmout_ref[...], preferred_element_type=F32)

    g0 = jax.nn.sigmoid(proj(cols.gates, d_model))
    g2 = jax.nn.sigmoid(proj(cols.gates + 2 * d_model, d_model))
    part_ref[0] = (g0 * ya + g2 * ym).astype(BF16)
    g1_ref[0] = jax.nn.sigmoid(proj(cols.gates + d_model, d_model)).astype(BF16)

    qT = proj(cols.q, N_HEADS * HEAD_PAD).T
    row8 = lax.broadcasted_iota(jnp.int32, (SUBLANE, tm), 0)
    zpad = jnp.zeros((HEAD_PAD - HEAD_DIM - SUBLANE, tm), F32)
    for h in range(N_HEADS):
        blk = qT[h * HEAD_PAD:h * HEAD_PAD + HEAD_DIM, :]
        r = lax.rsqrt(jnp.sum(blk * blk, axis=0, keepdims=True) * (1.0 / HEAD_DIM) + EPS)
        slope = 2.0 ** (-8.0 * (h + 1) / N_HEADS)
        aug = jnp.where(row8 < 2, slope, 0.0).astype(F32)
        full = jnp.concatenate([blk * r * gqT_ref[...], aug, zpad], axis=0)
        qaugT_ref[0, h * HEAD_PAD:(h + 1) * HEAD_PAD, :] = full.astype(BF16)

    kraw = proj(cols.k, N_HEADS * HEAD_PAD)
    posf = (j * tm + lax.broadcasted_iota(jnp.int32, (tm, HEAD_PAD), 0)).astype(F32)
    lane = lax.broadcasted_iota(jnp.int32, (tm, HEAD_PAD), 1)
    pos_hi = jnp.floor(posf * (1.0 / POS_SPLIT)) * POS_SPLIT
    posmat = jnp.where(lane == HEAD_DIM, pos_hi, jnp.where(lane == HEAD_DIM + 1, posf - pos_hi, 0.0))
    for h in range(N_HEADS):
        kg = kraw[:, h * HEAD_PAD:(h + 1) * HEAD_PAD]
        r = lax.rsqrt(jnp.sum(kg * kg, axis=-1, keepdims=True) * (1.0 / HEAD_DIM) + EPS)
        kaug_ref[0, :, h * HEAD_PAD:(h + 1) * HEAD_PAD] = (kg * r * gk_ref[...] + posmat).astype(BF16)

    vT_ref[0] = proj(cols.v, N_HEADS * HEAD_DIM).T.astype(BF16)
    qiT_ref[0] = proj(cols.qi, IDX_HEADS * IDX_DIM).T.astype(BF16)
    kw = proj(cols.kw, LANE)
    kw_ref[0] = kw.astype(BF16)
    wiT_ref[0] = kw.T[IDX_DIM:IDX_DIM + IDX_HEADS, :]


def _in_proj(x, g_mix, w_perm, conv_a_w, w_a_out_bf, gk_pad, gqT, kmT, vm, mq_g, w_m_out_bf, tm):
    B, T, D = x.shape
    cols = _InCols(D)
    M = vm.shape[1]
    memw = MEM_HEADS * MEM_HEAD_DIM
    tok = lambda w: pl.BlockSpec((1, tm, w), lambda b, j: (b, j, 0))
    tokT = lambda w: pl.BlockSpec((1, w, tm), lambda b, j: (b, 0, j))
    return pl.pallas_call(
        functools.partial(_in_proj_kernel, cols, tm, D),
        grid=(B, T // tm),
        in_specs=[
            tok(D),
            _const_spec((1, D)),
            _const_spec((D, cols.total)),
            _const_spec((3, cols.sc)),
            _const_spec((cols.sc, D)),
            _const_spec((1, HEAD_PAD)),
            _const_spec((HEAD_DIM, tm)),
            pl.BlockSpec((1, memw, M), lambda b, j: (b, 0, 0)),
            pl.BlockSpec((1, M, memw), lambda b, j: (b, 0, 0)),
            _const_spec((1, MEM_HEAD_DIM)),
            _const_spec((memw, D)),
        ],
        out_specs=[
            tok(D), tok(D),
            tokT(N_HEADS * HEAD_PAD), tok(N_HEADS * HEAD_PAD),
            tokT(N_HEADS * HEAD_DIM), tokT(IDX_HEADS * IDX_DIM),
            tok(LANE), tokT(IDX_HEADS),
        ],
        out_shape=[
            jax.ShapeDtypeStruct((B, T, D), BF16),
            jax.ShapeDtypeStruct((B, T, D), BF16),
            jax.ShapeDtypeStruct((B, N_HEADS * HEAD_PAD, T), BF16),
            jax.ShapeDtypeStruct((B, T, N_HEADS * HEAD_PAD), BF16),
            jax.ShapeDtypeStruct((B, N_HEADS * HEAD_DIM, T), BF16),
            jax.ShapeDtypeStruct((B, IDX_HEADS * IDX_DIM, T), BF16),
            jax.ShapeDtypeStruct((B, T, LANE), BF16),
            jax.ShapeDtypeStruct((B, IDX_HEADS, T), F32),
        ],
        scratch_shapes=[pltpu.VMEM((SUBLANE, cols.sc), F32)],
        compiler_params=pltpu.CompilerParams(
            dimension_semantics=("arbitrary", "arbitrary"), vmem_limit_bytes=VMEM_LIMIT),
        name="in_proj",
    )(x, g_mix, w_perm, conv_a_w, w_a_out_bf, gk_pad, gqT, kmT, vm, mq_g, w_m_out_bf)


def _dsa_kernel(k_sel, n_keys, kw_ref, qiT_ref, wiT_ref, kaug_ref, qaugT_ref, vT_ref, out_ref,
                s_ref, lg_ref, acc_ref, jlim_ref):
    i = pl.program_id(1)
    n_tiles = i + 1
    qpos = i * QB + lax.broadcasted_iota(jnp.int32, (1, QB), 1)
    row = lax.broadcasted_iota(jnp.int32, (KT, QB), 0)
    groups = KT // SUBLANE
    inf = jnp.float32(jnp.inf)

    def tile_start(t):
        return pl.multiple_of(t * KT, KT)

    zrows = jnp.zeros((LANE - IDX_DIM, QB), BF16)

    def score_tile(t, carry):
        lo_part, hi_part = carry
        r0 = tile_start(t)
        kw_t = kw_ref[0, pl.ds(r0, KT), :]
        acc = jnp.zeros((KT, QB), F32)
        for h in range(IDX_HEADS):
            rhs = jnp.concatenate([qiT_ref[0, h * IDX_DIM:(h + 1) * IDX_DIM, :], zrows], axis=0)
            lg = jnp.dot(kw_t, rhs, preferred_element_type=F32)
            acc = acc + wiT_ref[0, h:h + 1, :] * jnp.maximum(lg, 0.0)
        valid = r0 + row <= qpos
        s_ref[pl.ds(r0, KT), :] = jnp.where(valid, acc, -inf)
        lo_t = jnp.min(jnp.where(valid, acc, inf).reshape(groups, SUBLANE, QB), axis=0)
        hi_t = jnp.max(jnp.where(valid, acc, -inf).reshape(groups, SUBLANE, QB), axis=0)
        return jnp.minimum(lo_part, lo_t), jnp.maximum(hi_part, hi_t)

    lo_part, hi_part = lax.fori_loop(
        0, n_tiles, score_tile,
        (jnp.full((SUBLANE, QB), jnp.inf, F32), jnp.full((SUBLANE, QB), -jnp.inf, F32)))

    def column_pass(tile_fn, combine, init):
        def body(t, part):
            r0 = tile_start(t)
            s3 = s_ref[pl.ds(r0, KT), :].reshape(groups, SUBLANE, QB)
            return combine(part, tile_fn(s3, r0))
        part = lax.fori_loop(0, n_tiles, body, jnp.full((SUBLANE, QB), init, F32))
        return part

    def bcast8(v):
        return jnp.broadcast_to(v, (SUBLANE, QB))[None]

    def count_where(indicator_fn):
        part = column_pass(lambda s3, r0: jnp.sum(indicator_fn(s3, r0), axis=0),
                           lambda a, b: a + b, 0.0)
        return jnp.sum(part, axis=0, keepdims=True)

    def count_ge(v):
        v8 = bcast8(v)
        return count_where(lambda s3, r0: jnp.where(s3 >= v8, 1.0, 0.0))

    def count_gt(v):
        v8 = bcast8(v)
        return count_where(lambda s3, r0: jnp.where(s3 > v8, 1.0, 0.0))

    def min_ge(v):
        v8 = bcast8(v)
        part = column_pass(lambda s3, r0: jnp.min(jnp.where(s3 >= v8, s3, inf), axis=0),
                           jnp.minimum, jnp.inf)
        return jnp.min(part, axis=0, keepdims=True)

    kf = jnp.minimum(qpos + 1, k_sel).astype(F32)
    hi0 = jnp.max(hi_part, axis=0, keepdims=True)
    lo0 = jnp.min(lo_part, axis=0, keepdims=True)
    clo0 = (qpos + 1).astype(F32)

    def bisect(_, st):
        lo, hi, clo = st
        mid = lo + (hi - lo) * 0.5
        mid = jnp.where(mid <= lo, hi, mid)
        c = count_ge(mid)
        ge = c >= kf
        return jnp.where(ge, mid, lo), jnp.where(ge, hi, mid), jnp.where(ge, c, clo)

    def snap(lo):
        tau = min_ge(lo)
        return tau, count_gt(tau)

    def pending(clo, cgt):
        done = jnp.logical_or(clo == kf, cgt < kf)
        return jnp.max(jnp.where(done, 0.0, 1.0))

    lo, hi, clo = lax.fori_loop(0, BISECT_FIXED, bisect, (lo0, hi0, clo0))
    tau, cgt = snap(lo)

    def refine_cond(st):
        return st[5] > 0.0

    def refine(st):
        lo, hi, clo = lax.fori_loop(0, BISECT_ROUND, bisect, st[:3])
        tau, cgt = snap(lo)
        return lo, hi, clo, tau, cgt, pending(clo, cgt)

    lo, hi, clo, tau, cgt, _ = lax.while_loop(
        refine_cond, refine, (lo, hi, clo, tau, cgt, pending(clo, cgt)))

    need = kf - cgt
    excess = clo > kf
    jlim_ref[...] = jnp.full((1, QB), float(n_keys), F32)

    @pl.when(jnp.max(jnp.where(excess, 1.0, 0.0)) > 0.0)
    def _():
        tau8 = bcast8(tau)
        pos3 = lax.broadcasted_iota(jnp.int32, (groups, SUBLANE, QB), 0) * SUBLANE + \
            lax.broadcasted_iota(jnp.int32, (groups, SUBLANE, QB), 1)

        def step(_, st):
            jlo, jhi = st
            jmid = jnp.floor((jlo + jhi) * 0.5)
            j8 = bcast8(jmid)
            f = count_where(lambda s3, r0: jnp.where(
                s3 == tau8, jnp.where((r0 + pos3).astype(F32) <= j8, 1.0, 0.0), 0.0))
            ok = f >= need
            return jnp.where(ok, jlo, jmid), jnp.where(ok, jmid, jhi)

        _, jhi = lax.fori_loop(
            0, n_keys.bit_length(), step,
            (jnp.full((1, QB), -1.0, F32), jnp.full((1, QB), float(n_keys - 1), F32)))
        jlim_ref[...] = jnp.where(excess, jhi, jlim_ref[...])

    jlim = jlim_ref[...]

    def logits_tile(t, mparts):
        r0 = tile_start(t)
        s_t = s_ref[pl.ds(r0, KT), :]
        kposf = (r0 + row).astype(F32)
        bias = jnp.where(s_t > tau, 0.0,
                         jnp.where(s_t == tau, jnp.where(kposf <= jlim, 0.0, -inf), -inf))
        out = []
        for h in range(N_HEADS):
            hs = slice(h * HEAD_PAD, (h + 1) * HEAD_PAD)
            s = jnp.dot(kaug_ref[0, pl.ds(r0, KT), hs], qaugT_ref[0, hs, :],
                        preferred_element_type=F32) + bias
            lg_ref[h, pl.ds(r0, KT), :] = s
            out.append(jnp.maximum(mparts[h], jnp.max(s.reshape(groups, SUBLANE, QB), axis=0)))
        return tuple(out)

    mparts = lax.fori_loop(0, n_tiles, logits_tile,
                           tuple(jnp.full((SUBLANE, QB), NEG_BIG, F32) for _ in range(N_HEADS)))
    m_rows = [jnp.max(mp, axis=0, keepdims=True) for mp in mparts]

    acc_ref[...] = jnp.zeros(acc_ref.shape, F32)

    def pv_tile(t, lparts):
        r0 = tile_start(t)
        out = []
        for h in range(N_HEADS):
            vs = slice(h * HEAD_DIM, (h + 1) * HEAD_DIM)
            p = jnp.exp(lg_ref[h, pl.ds(r0, KT), :] - m_rows[h])
            out.append(lparts[h] + jnp.sum(p.reshape(groups, SUBLANE, QB), axis=0))
            acc_ref[vs, :] += jnp.dot(vT_ref[0, vs, pl.ds(r0, KT)], p.astype(BF16),
                                      preferred_element_type=F32)
        return tuple(out)

    lparts = lax.fori_loop(0, n_tiles, pv_tile,
                           tuple(jnp.zeros((SUBLANE, QB), F32) for _ in range(N_HEADS)))

    for h in range(N_HEADS):
        vs = slice(h * HEAD_DIM, (h + 1) * HEAD_DIM)
        denom = jnp.sum(lparts[h], axis=0, keepdims=True)
        out_ref[0, vs, :] = (acc_ref[vs, :] / denom).astype(BF16)


def _dsa(kw, qiT, wiT, kaug, qaugT, vT, k_sel):
    B, T, _ = kaug.shape
    assert T % QB == 0 and QB == KT
    att = N_HEADS * HEAD_DIM
    qblk = lambda w: pl.BlockSpec((1, w, QB), lambda b, i: (b, 0, i))
    return pl.pallas_call(
        functools.partial(_dsa_kernel, k_sel, T),
        grid=(B, T // QB),
        in_specs=[
            pl.BlockSpec((1, T, LANE), lambda b, i: (b, 0, 0)),
            qblk(IDX_HEADS * IDX_DIM),
            qblk(IDX_HEADS),
            pl.BlockSpec((1, T, N_HEADS * HEAD_PAD), lambda b, i: (b, 0, 0)),
            qblk(N_HEADS * HEAD_PAD),
            pl.BlockSpec((1, att, T), lambda b, i: (b, 0, 0)),
        ],
        out_specs=qblk(att),
        out_shape=jax.ShapeDtypeStruct((B, att, T), BF16),
        scratch_shapes=[
            pltpu.VMEM((T, QB), F32),
            pltpu.VMEM((N_HEADS, T, QB), F32),
            pltpu.VMEM((att, QB), F32),
            pltpu.VMEM((1, QB), F32),
        ],
        compiler_params=pltpu.CompilerParams(
            dimension_semantics=("arbitrary", "arbitrary"), vmem_limit_bytes=VMEM_LIMIT),
        name="dsa",
    )(kw, qiT, wiT, kaug, qaugT, vT)


def _out_ffn_kernel(tm, d_ff, x_ref, part_ref, g1_ref, attnT_ref, wb_ref, wo_ref, gffn_ref,
                    wup_ref, convf_ref, wdown_ref, out_ref, halo_ref):
    j = pl.program_id(1)
    yb = lax.dot_general(attnT_ref[0], wb_ref[...], (((0,), (0,)), ((), ())),
                         preferred_element_type=F32)
    merged = (part_ref[0].astype(F32) + g1_ref[0].astype(F32) * yb).astype(BF16)
    x1 = x_ref[0] + jnp.dot(merged, wo_ref[...], preferred_element_type=F32)
    xn2 = _rms_rows(x1, gffn_ref[...]).astype(BF16)

    @pl.when(j == 0)
    def _():
        halo_ref[...] = jnp.zeros_like(halo_ref)

    def conv_up(off):
        cs = slice(off, off + FF_CHUNK)
        up = jnp.dot(xn2, wup_ref[:, cs], preferred_element_type=F32)
        out = _causal_conv3(up, halo_ref[:, cs], convf_ref[:, cs])
        halo_ref[:, cs] = up[tm - SUBLANE:tm, :]
        return out

    acc = x1
    for c in range(d_ff // FF_CHUNK):
        gate = conv_up(c * FF_CHUNK)
        val = conv_up(d_ff + c * FF_CHUNK)
        hmid = (gate * jax.nn.sigmoid(gate) * val).astype(BF16)
        acc = acc + jnp.dot(hmid, wdown_ref[c * FF_CHUNK:(c + 1) * FF_CHUNK, :],
                            preferred_element_type=F32)
    out_ref[0] = acc


def _out_ffn(x, part, g1, attnT, w_b_bf, w_o_bf, g_ffn, w_up_bf, conv_f_w, w_down_bf, tm):
    B, T, D = x.shape
    d_ff = w_down_bf.shape[0]
    att = N_HEADS * HEAD_DIM
    assert d_ff % FF_CHUNK == 0
    tok = lambda w: pl.BlockSpec((1, tm, w), lambda b, j: (b, j, 0))
    return pl.pallas_call(
        functools.partial(_out_ffn_kernel, tm, d_ff),
        grid=(B, T // tm),
        in_specs=[
            tok(D), tok(D), tok(D),
            pl.BlockSpec((1, att, tm), lambda b, j: (b, 0, j)),
            _const_spec((att, D)),
            _const_spec((D, D)),
            _const_spec((1, D)),
            _const_spec((D, 2 * d_ff)),
            _const_spec((3, 2 * d_ff)),
            _const_spec((d_ff, D)),
        ],
        out_specs=tok(D),
        out_shape=jax.ShapeDtypeStruct((B, T, D), F32),
        scratch_shapes=[pltpu.VMEM((SUBLANE, 2 * d_ff), F32)],
        compiler_params=pltpu.CompilerParams(
            dimension_semantics=("arbitrary", "arbitrary"), vmem_limit_bytes=VMEM_LIMIT),
        name="out_ffn",
    )(x, part, g1, attnT, w_b_bf, w_o_bf, g_ffn, w_up_bf, conv_f_w, w_down_bf)


def _layer(x, mem, g_mix, w_in, conv_a_w, w_a_out, q_norm_g, k_norm_g, w_b_out, g_mem, w_mem_kv,
           mq_norm_g, mk_norm_g, w_m_out, w_o, g_ffn, w_up, conv_f_w, w_down):
    B, T, D = x.shape
    tm_in = min(TM_IN, T)
    tm_out = min(TM_OUT, T)
    k_sel = min(TOPK_MAX, T // 4)
    row = lambda g: g.reshape(1, -1).astype(F32)

    kmT, vm = _mem_kv(mem, row(g_mem), w_mem_kv.astype(BF16), row(mk_norm_g))

    scale = HEAD_DIM ** -0.5
    gk_pad = jnp.pad(row(k_norm_g), ((0, 0), (0, HEAD_PAD - HEAD_DIM)))
    gqT = jnp.broadcast_to((q_norm_g.astype(F32) * scale)[:, None], (HEAD_DIM, tm_in))
    part, g1, qaugT, kaug, vT, qiT, kw, wiT = _in_proj(
        x, row(g_mix), _relayout_w_in(w_in, D), conv_a_w, w_a_out.astype(BF16), gk_pad, gqT,
        kmT, vm, row(mq_norm_g), w_m_out.astype(BF16), tm_in)

    attnT = _dsa(kw, qiT, wiT, kaug, qaugT, vT, k_sel)

    return _out_ffn(x, part, g1, attnT, w_b_out.astype(BF16), w_o.astype(BF16), row(g_ffn),
                    w_up.astype(BF16), conv_f_w, w_down.astype(BF16), tm_out)


def kernel(x, mem, g_mix, w_in, conv_a_w, w_a_out, q_norm_g, k_norm_g, w_b_out, g_mem, w_mem_kv,
           mq_norm_g, mk_norm_g, w_m_out, w_o, g_ffn, w_up, conv_f_w, w_down):
    for l in range(g_mix.shape[0]):
        x = _layer(x, mem, g_mix[l], w_in[l], conv_a_w[l], w_a_out[l], q_norm_g[l], k_norm_g[l],
                   w_b_out[l], g_mem[l], w_mem_kv[l], mq_norm_g[l], mk_norm_g[l], w_m_out[l],
                   w_o[l], g_ffn[l], w_up[l], conv_f_w[l], w_down[l])
    return x
```

```python
import functools

import jax
import jax.numpy as jnp
from jax import lax
from jax.experimental import pallas as pl
from jax.experimental.pallas import tpu as pltpu

F32 = jnp.float32
BF16 = jnp.bfloat16

EPS = 1e-6
N_HEADS = 8
HEAD_DIM = 64
IDX_HEADS = 8
IDX_DIM = 32
MEM_HEADS = 4
MEM_HEAD_DIM = 128
TOPK_MAX = 256
N_BRANCH = 3

LANE = 128
SUBLANE = 8
HEAD_PAD = LANE
POS_SPLIT = 8.0
VMEM_LIMIT = 60 * 1024 * 1024

TM_IN = 512
TM_OUT = 512
QB = 256
KT = 256
FF_CHUNK = 256
FF_SLOTS = 4
BISECT_FIXED = 11
NEG_BIG = -1e30


def _const_spec(shape):
    nd = len(shape)
    return pl.BlockSpec(shape, lambda *_: (0,) * nd, pipeline_mode=pl.Buffered(1))


def _causal_conv3(u, prev8, w3, buf_ref, slot):
    rows = u.shape[0]
    outs = []
    for c in range(u.shape[1] // LANE):
        cs = slice(c * LANE, (c + 1) * LANE)
        buf_ref[slot + c, 0:SUBLANE, :] = prev8[:, cs]
        buf_ref[slot + c, SUBLANE:SUBLANE + rows, :] = u[:, cs]
        u1 = buf_ref[slot + c, pl.ds(SUBLANE - 1, rows), :]
        u2 = buf_ref[slot + c, pl.ds(SUBLANE - 2, rows), :]
        outs.append(w3[0:1, cs] * u2 + w3[1:2, cs] * u1 + w3[2:3, cs] * u[:, cs])
    return jnp.concatenate(outs, axis=1)


def _rms_rows(x, g):
    return x * lax.rsqrt(jnp.mean(x * x, axis=-1, keepdims=True) + EPS) * g


def _mem_kv_kernel(mem_ref, gmem_ref, wkv_ref, gmk_ref, kmT_ref, vm_ref):
    memn = _rms_rows(mem_ref[0], gmem_ref[...]).astype(BF16)
    width = MEM_HEADS * MEM_HEAD_DIM
    km = jnp.dot(memn, wkv_ref[:, 0:width], preferred_element_type=F32)
    vm = jnp.dot(memn, wkv_ref[:, width:2 * width], preferred_element_type=F32)
    heads = []
    for h in range(MEM_HEADS):
        kh = km[:, h * MEM_HEAD_DIM:(h + 1) * MEM_HEAD_DIM]
        heads.append(_rms_rows(kh, gmk_ref[...]))
    kmT_ref[0] = jnp.concatenate(heads, axis=1).T.astype(BF16)
    vm_ref[0] = vm.astype(BF16)


def _mem_kv(mem, g_mem, w_kv_bf, mk_g):
    B, M, D = mem.shape
    width = MEM_HEADS * MEM_HEAD_DIM
    return pl.pallas_call(
        _mem_kv_kernel,
        grid=(B,),
        in_specs=[
            pl.BlockSpec((1, M, D), lambda b: (b, 0, 0)),
            _const_spec((1, D)),
            _const_spec((D, 2 * width)),
            _const_spec((1, MEM_HEAD_DIM)),
        ],
        out_specs=[
            pl.BlockSpec((1, width, M), lambda b: (b, 0, 0)),
            pl.BlockSpec((1, M, width), lambda b: (b, 0, 0)),
        ],
        out_shape=[
            jax.ShapeDtypeStruct((B, width, M), BF16),
            jax.ShapeDtypeStruct((B, M, width), BF16),
        ],
        compiler_params=pltpu.CompilerParams(
            dimension_semantics=("arbitrary",), vmem_limit_bytes=VMEM_LIMIT),
        name="mem_kv",
    )(mem, g_mem, w_kv_bf, mk_g)


class _InCols:
    def __init__(self, d_model):
        sc = d_model // 2
        self.sc = sc
        self.bch = 0
        self.q = self.bch + 3 * sc
        self.k = self.q + N_HEADS * HEAD_DIM
        self.v = self.k + N_HEADS * HEAD_PAD
        self.qi = self.v + N_HEADS * HEAD_DIM
        self.kw = self.qi + IDX_HEADS * IDX_DIM
        self.qm = self.kw + LANE
        self.gates = self.qm + MEM_HEADS * MEM_HEAD_DIM
        self.total = self.gates + N_BRANCH * d_model


def _relayout_w_in(w_in, d_model):
    sc = d_model // 2
    att = N_HEADS * HEAD_DIM
    splits = [sc, sc, sc, att, att, att, IDX_HEADS * IDX_DIM, IDX_DIM, IDX_HEADS,
              MEM_HEADS * MEM_HEAD_DIM, N_BRANCH * d_model]
    offs = [0]
    for s in splits:
        offs.append(offs[-1] + s)
    piece = lambda i: w_in[:, offs[i]:offs[i + 1]]

    def pad_heads(w):
        w = w.reshape(d_model, N_HEADS, HEAD_DIM)
        w = jnp.pad(w, ((0, 0), (0, 0), (0, HEAD_PAD - HEAD_DIM)))
        return w.reshape(d_model, N_HEADS * HEAD_PAD)

    kw = jnp.pad(jnp.concatenate([piece(7), piece(8)], axis=1),
                 ((0, 0), (0, LANE - IDX_DIM - IDX_HEADS)))
    cols = [piece(0), piece(1), piece(2), piece(3), pad_heads(piece(4)), piece(5),
            piece(6), kw, piece(9), piece(10)]
    return jnp.concatenate(cols, axis=1).astype(BF16)


def _in_proj_kernel(cols, tm, d_model,
                    x_ref, gmix_ref, w_ref, convw_ref, waout_ref, gk_ref, gqT_ref,
                    kmT_ref, vm_ref, gmq_ref, wmout_ref,
                    part_ref, g1_ref, qaugT_ref, kaug_ref, vT_ref, qiT_ref, kw_ref, wiT_ref,
                    halo_ref, cbuf_ref):
    j = pl.program_id(1)
    sc = cols.sc
    xn = _rms_rows(x_ref[0], gmix_ref[...]).astype(BF16)

    def proj(off, width):
        return jnp.dot(xn, w_ref[:, off:off + width], preferred_element_type=F32)

    @pl.when(j == 0)
    def _():
        halo_ref[...] = jnp.zeros_like(halo_ref)

    ch = proj(cols.bch + sc, sc) * proj(cols.bch + 2 * sc, sc)
    conv = _causal_conv3(ch, halo_ref[...], convw_ref[...], cbuf_ref, 0)
    halo_ref[...] = ch[tm - SUBLANE:tm, :]
    ua = (proj(cols.bch, sc) * conv).astype(BF16)
    ya = jnp.dot(ua, waout_ref[...], preferred_element_type=F32)

    qm = proj(cols.qm, MEM_HEADS * MEM_HEAD_DIM)
    mem_scale = MEM_HEAD_DIM ** -0.5
    heads = []
    for h in range(MEM_HEADS):
        sl = slice(h * MEM_HEAD_DIM, (h + 1) * MEM_HEAD_DIM)
        qh = _rms_rows(qm[:, sl], gmq_ref[...]).astype(BF16)
        lg = jnp.dot(qh, kmT_ref[0, sl, :], preferred_element_type=F32) * mem_scale
        p = jnp.exp(lg - jnp.max(lg, axis=-1, keepdims=True))
        denom = jnp.sum(p, axis=-1, keepdims=True)
        oh = jnp.dot(p.astype(BF16), vm_ref[0, :, sl], preferred_element_type=F32)
        heads.append(oh / denom)
    om = jnp.concatenate(heads, axis=1).astype(BF16)
    ym = jnp.dot(om, wmout_ref[...], preferred_element_type=F32)

    g0 = jax.nn.sigmoid(proj(cols.gates, d_model))
    g2 = jax.nn.sigmoid(proj(cols.gates + 2 * d_model, d_model))
    part_ref[0] = (g0 * ya + g2 * ym).astype(BF16)
    g1_ref[0] = jax.nn.sigmoid(proj(cols.gates + d_model, d_model)).astype(BF16)

    qT = proj(cols.q, N_HEADS * HEAD_DIM).T
    row8 = lax.broadcasted_iota(jnp.int32, (SUBLANE, tm), 0)
    zpad = jnp.zeros((HEAD_PAD - HEAD_DIM - SUBLANE, tm), F32)
    for h in range(N_HEADS):
        blk = qT[h * HEAD_DIM:(h + 1) * HEAD_DIM, :]
        r = lax.rsqrt(jnp.sum(blk * blk, axis=0, keepdims=True) * (1.0 / HEAD_DIM) + EPS)
        slope = 2.0 ** (-8.0 * (h + 1) / N_HEADS)
        aug = jnp.where(row8 < 2, slope, 0.0).astype(F32)
        full = jnp.concatenate([blk * r * gqT_ref[...], aug, zpad], axis=0)
        qaugT_ref[0, h * HEAD_PAD:(h + 1) * HEAD_PAD, :] = full.astype(BF16)

    kraw = proj(cols.k, N_HEADS * HEAD_PAD)
    posf = (j * tm + lax.broadcasted_iota(jnp.int32, (tm, HEAD_PAD), 0)).astype(F32)
    lane = lax.broadcasted_iota(jnp.int32, (tm, HEAD_PAD), 1)
    pos_hi = jnp.floor(posf * (1.0 / POS_SPLIT)) * POS_SPLIT
    posmat = jnp.where(lane == HEAD_DIM, pos_hi, jnp.where(lane == HEAD_DIM + 1, posf - pos_hi, 0.0))
    for h in range(N_HEADS):
        kg = kraw[:, h * HEAD_PAD:(h + 1) * HEAD_PAD]
        r = lax.rsqrt(jnp.sum(kg * kg, axis=-1, keepdims=True) * (1.0 / HEAD_DIM) + EPS)
        kaug_ref[0, :, h * HEAD_PAD:(h + 1) * HEAD_PAD] = (kg * r * gk_ref[...] + posmat).astype(BF16)

    vT_ref[0] = proj(cols.v, N_HEADS * HEAD_DIM).T.astype(BF16)
    qiT_ref[0] = proj(cols.qi, IDX_HEADS * IDX_DIM).T.astype(BF16)
    kw = proj(cols.kw, LANE)
    kw_ref[0] = kw.astype(BF16)
    wiT_ref[0] = kw.T[IDX_DIM:IDX_DIM + IDX_HEADS, :]


def _in_proj(x, g_mix, w_perm, conv_a_w, w_a_out_bf, gk_pad, gqT, kmT, vm, mq_g, w_m_out_bf, tm):
    B, T, D = x.shape
    cols = _InCols(D)
    M = vm.shape[1]
    memw = MEM_HEADS * MEM_HEAD_DIM
    tok = lambda w: pl.BlockSpec((1, tm, w), lambda b, j: (b, j, 0))
    tokT = lambda w: pl.BlockSpec((1, w, tm), lambda b, j: (b, 0, j))
    return pl.pallas_call(
        functools.partial(_in_proj_kernel, cols, tm, D),
        grid=(B, T // tm),
        in_specs=[
            tok(D),
            _const_spec((1, D)),
            _const_spec((D, cols.total)),
            _const_spec((3, cols.sc)),
            _const_spec((cols.sc, D)),
            _const_spec((1, HEAD_PAD)),
            _const_spec((HEAD_DIM, tm)),
            pl.BlockSpec((1, memw, M), lambda b, j: (b, 0, 0)),
            pl.BlockSpec((1, M, memw), lambda b, j: (b, 0, 0)),
            _const_spec((1, MEM_HEAD_DIM)),
            _const_spec((memw, D)),
        ],
        out_specs=[
            tok(D), tok(D),
            tokT(N_HEADS * HEAD_PAD), tok(N_HEADS * HEAD_PAD),
            tokT(N_HEADS * HEAD_DIM), tokT(IDX_HEADS * IDX_DIM),
            tok(LANE), tokT(IDX_HEADS),
        ],
        out_shape=[
            jax.ShapeDtypeStruct((B, T, D), BF16),
            jax.ShapeDtypeStruct((B, T, D), BF16),
            jax.ShapeDtypeStruct((B, N_HEADS * HEAD_PAD, T), BF16),
            jax.ShapeDtypeStruct((B, T, N_HEADS * HEAD_PAD), BF16),
            jax.ShapeDtypeStruct((B, N_HEADS * HEAD_DIM, T), BF16),
            jax.ShapeDtypeStruct((B, IDX_HEADS * IDX_DIM, T), BF16),
            jax.ShapeDtypeStruct((B, T, LANE), BF16),
            jax.ShapeDtypeStruct((B, IDX_HEADS, T), F32),
        ],
        scratch_shapes=[pltpu.VMEM((SUBLANE, cols.sc), F32),
                        pltpu.VMEM((cols.sc // LANE, tm + SUBLANE, LANE), F32)],
        compiler_params=pltpu.CompilerParams(
            dimension_semantics=("arbitrary", "arbitrary"), vmem_limit_bytes=VMEM_LIMIT),
        name="in_proj",
    )(x, g_mix, w_perm, conv_a_w, w_a_out_bf, gk_pad, gqT, kmT, vm, mq_g, w_m_out_bf)


def _dsa_kernel(k_sel, n_keys, kw_ref, qiT_ref, wiT_ref, kaug_ref, qaugT_ref, vT_ref, out_ref,
                s_ref, lg_ref, acc_ref, jlim_ref):
    i = pl.program_id(1)
    n_tiles = i + 1
    qpos = i * QB + lax.broadcasted_iota(jnp.int32, (1, QB), 1)
    row = lax.broadcasted_iota(jnp.int32, (KT, QB), 0)
    groups = KT // SUBLANE
    inf = jnp.float32(jnp.inf)

    def tile_start(t):
        return pl.multiple_of(t * KT, KT)

    zrows = jnp.zeros((LANE - IDX_DIM, QB), BF16)

    def score_tile(t, carry):
        lo_part, hi_part = carry
        r0 = tile_start(t)
        kw_t = kw_ref[0, pl.ds(r0, KT), :]
        acc = jnp.zeros((KT, QB), F32)
        for h in range(IDX_HEADS):
            rhs = jnp.concatenate([qiT_ref[0, h * IDX_DIM:(h + 1) * IDX_DIM, :], zrows], axis=0)
            lg = jnp.dot(kw_t, rhs, preferred_element_type=F32)
            acc = acc + wiT_ref[0, h:h + 1, :] * jnp.maximum(lg, 0.0)
        valid = r0 + row <= qpos
        s_ref[pl.ds(r0, KT), :] = jnp.where(valid, acc, -inf)
        lo_t = jnp.min(jnp.where(valid, acc, inf).reshape(groups, SUBLANE, QB), axis=0)
        hi_t = jnp.max(jnp.where(valid, acc, -inf).reshape(groups, SUBLANE, QB), axis=0)
        return jnp.minimum(lo_part, lo_t), jnp.maximum(hi_part, hi_t)

    lo_part, hi_part = lax.fori_loop(
        0, n_tiles, score_tile,
        (jnp.full((SUBLANE, QB), jnp.inf, F32), jnp.full((SUBLANE, QB), -jnp.inf, F32)))

    def tree(x3, op):
        while x3.shape[0] > 1:
            half = x3.shape[0] // 2
            x3 = op(x3[:half], x3[half:])
        return x3[0]

    def load3(t):
        return s_ref[pl.ds(tile_start(t), KT), :].reshape(groups, SUBLANE, QB)

    def bcast8(v):
        return jnp.broadcast_to(v, (SUBLANE, QB))[None]

    def count_where(indicator_fn):
        def body(t, part):
            return part + tree(indicator_fn(load3(t), tile_start(t)), jnp.add)
        part = lax.fori_loop(0, n_tiles, body, jnp.zeros((SUBLANE, QB), F32))
        return jnp.sum(part, axis=0, keepdims=True)

    def count_ge(v):
        v8 = bcast8(v)
        return count_where(lambda s3, r0: jnp.where(s3 >= v8, 1.0, 0.0))

    def bracket_ends(lo, hi):
        lo8, hi8 = bcast8(lo), bcast8(hi)

        def body(t, carry):
            s3 = load3(t)
            a_t = tree(jnp.where(s3 >= lo8, s3, inf), jnp.minimum)
            b_t = tree(jnp.where(s3 < hi8, s3, -inf), jnp.maximum)
            return jnp.minimum(carry[0], a_t), jnp.maximum(carry[1], b_t)

        a_part, b_part = lax.fori_loop(
            0, n_tiles, body,
            (jnp.full((SUBLANE, QB), jnp.inf, F32), jnp.full((SUBLANE, QB), -jnp.inf, F32)))
        return jnp.min(a_part, axis=0, keepdims=True), jnp.max(b_part, axis=0, keepdims=True)

    kf = jnp.minimum(qpos + 1, k_sel).astype(F32)
    top = jnp.max(hi_part, axis=0, keepdims=True)
    lo0 = jnp.min(lo_part, axis=0, keepdims=True)
    hi0 = top + (top - lo0) + jnp.maximum(jnp.abs(top), 1.0) * 1e-3
    clo0 = (qpos + 1).astype(F32)

    def split(st, mid):
        lo, hi, clo, chi = st
        c = count_ge(mid)
        ge = c >= kf
        return (jnp.where(ge, mid, lo), jnp.where(ge, hi, mid),
                jnp.where(ge, c, clo), jnp.where(ge, chi, c))

    def bisect(_, st):
        lo, hi = st[0], st[1]
        mid = lo + (hi - lo) * 0.5
        return split(st, jnp.where(mid <= lo, hi, mid))

    st = lax.fori_loop(0, BISECT_FIXED, bisect, (lo0, hi0, clo0, jnp.zeros((1, QB), F32)))

    def unresolved(a, b):
        return jnp.max(jnp.where(a < b, 1.0, 0.0))

    def refine(carry):
        st, a, b = carry[:4], carry[4], carry[5]
        mid = a + (b - a) * 0.5
        st = split(st, jnp.where(mid <= a, b, mid))
        a, b = bracket_ends(st[0], st[1])
        return st + (a, b, unresolved(a, b))

    a0, b0 = bracket_ends(st[0], st[1])
    lo, hi, clo, cgt, tau, _, _ = lax.while_loop(
        lambda c: c[6] > 0.0, refine, st + (a0, b0, unresolved(a0, b0)))

    need = kf - cgt
    excess = clo > kf
    jlim_ref[...] = jnp.full((1, QB), float(n_keys), F32)

    @pl.when(jnp.max(jnp.where(excess, 1.0, 0.0)) > 0.0)
    def _():
        tau8 = bcast8(tau)
        pos3 = lax.broadcasted_iota(jnp.int32, (groups, SUBLANE, QB), 0) * SUBLANE + \
            lax.broadcasted_iota(jnp.int32, (groups, SUBLANE, QB), 1)

        def step(_, st):
            jlo, jhi = st
            jmid = jnp.floor((jlo + jhi) * 0.5)
            j8 = bcast8(jmid)
            f = count_where(lambda s3, r0: jnp.where(
                s3 == tau8, jnp.where((r0 + pos3).astype(F32) <= j8, 1.0, 0.0), 0.0))
            ok = f >= need
            return jnp.where(ok, jlo, jmid), jnp.where(ok, jmid, jhi)

        _, jhi = lax.fori_loop(
            0, n_keys.bit_length(), step,
            (jnp.full((1, QB), -1.0, F32), jnp.full((1, QB), float(n_keys - 1), F32)))
        jlim_ref[...] = jnp.where(excess, jhi, jlim_ref[...])

    jlim = jlim_ref[...]

    def logits_tile(t, mparts):
        r0 = tile_start(t)
        s_t = s_ref[pl.ds(r0, KT), :]
        kposf = (r0 + row).astype(F32)
        bias = jnp.where(s_t > tau, 0.0,
                         jnp.where(s_t == tau, jnp.where(kposf <= jlim, 0.0, -inf), -inf))
        out = []
        for h in range(N_HEADS):
            hs = slice(h * HEAD_PAD, (h + 1) * HEAD_PAD)
            s = jnp.dot(kaug_ref[0, pl.ds(r0, KT), hs], qaugT_ref[0, hs, :],
                        preferred_element_type=F32) + bias
            lg_ref[h, pl.ds(r0, KT), :] = s
            out.append(jnp.maximum(mparts[h], jnp.max(s.reshape(groups, SUBLANE, QB), axis=0)))
        return tuple(out)

    mparts = lax.fori_loop(0, n_tiles, logits_tile,
                           tuple(jnp.full((SUBLANE, QB), NEG_BIG, F32) for _ in range(N_HEADS)))
    m_rows = [jnp.max(mp, axis=0, keepdims=True) for mp in mparts]

    acc_ref[...] = jnp.zeros(acc_ref.shape, F32)

    def pv_tile(t, lparts):
        r0 = tile_start(t)
        out = []
        for h in range(N_HEADS):
            vs = slice(h * HEAD_DIM, (h + 1) * HEAD_DIM)
            p = jnp.exp(lg_ref[h, pl.ds(r0, KT), :] - m_rows[h])
            out.append(lparts[h] + jnp.sum(p.reshape(groups, SUBLANE, QB), axis=0))
            acc_ref[vs, :] += jnp.dot(vT_ref[0, vs, pl.ds(r0, KT)], p.astype(BF16),
                                      preferred_element_type=F32)
        return tuple(out)

    lparts = lax.fori_loop(0, n_tiles, pv_tile,
                           tuple(jnp.zeros((SUBLANE, QB), F32) for _ in range(N_HEADS)))

    for h in range(N_HEADS):
        vs = slice(h * HEAD_DIM, (h + 1) * HEAD_DIM)
        denom = jnp.sum(lparts[h], axis=0, keepdims=True)
        out_ref[0, vs, :] = (acc_ref[vs, :] / denom).astype(BF16)


def _dsa(kw, qiT, wiT, kaug, qaugT, vT, k_sel):
    B, T, _ = kaug.shape
    assert T % QB == 0 and QB == KT
    att = N_HEADS * HEAD_DIM
    qblk = lambda w: pl.BlockSpec((1, w, QB), lambda b, i: (b, 0, i))
    return pl.pallas_call(
        functools.partial(_dsa_kernel, k_sel, T),
        grid=(B, T // QB),
        in_specs=[
            pl.BlockSpec((1, T, LANE), lambda b, i: (b, 0, 0)),
            qblk(IDX_HEADS * IDX_DIM),
            qblk(IDX_HEADS),
            pl.BlockSpec((1, T, N_HEADS * HEAD_PAD), lambda b, i: (b, 0, 0)),
            qblk(N_HEADS * HEAD_PAD),
            pl.BlockSpec((1, att, T), lambda b, i: (b, 0, 0)),
        ],
        out_specs=qblk(att),
        out_shape=jax.ShapeDtypeStruct((B, att, T), BF16),
        scratch_shapes=[
            pltpu.VMEM((T, QB), F32),
            pltpu.VMEM((N_HEADS, T, QB), F32),
            pltpu.VMEM((att, QB), F32),
            pltpu.VMEM((1, QB), F32),
        ],
        compiler_params=pltpu.CompilerParams(
            dimension_semantics=("arbitrary", "arbitrary"), vmem_limit_bytes=VMEM_LIMIT),
        name="dsa",
    )(kw, qiT, wiT, kaug, qaugT, vT)


def _out_ffn_kernel(tm, d_ff, x_ref, part_ref, g1_ref, attnT_ref, wb_ref, wo_ref, gffn_ref,
                    wup_ref, convf_ref, wdown_ref, out_ref, halo_ref, cbuf_ref):
    j = pl.program_id(1)
    yb = lax.dot_general(attnT_ref[0], wb_ref[...], (((0,), (0,)), ((), ())),
                         preferred_element_type=F32)
    merged = (part_ref[0].astype(F32) + g1_ref[0].astype(F32) * yb).astype(BF16)
    x1 = x_ref[0] + jnp.dot(merged, wo_ref[...], preferred_element_type=F32)
    xn2 = _rms_rows(x1, gffn_ref[...]).astype(BF16)

    @pl.when(j == 0)
    def _():
        halo_ref[...] = jnp.zeros_like(halo_ref)

    def up_proj(c):
        return tuple(jnp.dot(xn2, wup_ref[:, off:off + FF_CHUNK], preferred_element_type=F32)
                     for off in (c * FF_CHUNK, d_ff + c * FF_CHUNK))

    def conv_chunk(up, off, slot):
        cs = slice(off, off + FF_CHUNK)
        out = _causal_conv3(up, halo_ref[:, cs], convf_ref[:, cs], cbuf_ref, slot)
        halo_ref[:, cs] = up[tm - SUBLANE:tm, :]
        return out

    n_chunks = d_ff // FF_CHUNK
    blocks = FF_CHUNK // LANE
    acc = x1
    ups = up_proj(0)
    for c in range(n_chunks):
        nxt = up_proj(c + 1) if c + 1 < n_chunks else None
        slot = (c % FF_SLOTS) * 2 * blocks
        gate = conv_chunk(ups[0], c * FF_CHUNK, slot)
        val = conv_chunk(ups[1], d_ff + c * FF_CHUNK, slot + blocks)
        hmid = (gate * jax.nn.sigmoid(gate) * val).astype(BF16)
        acc = acc + jnp.dot(hmid, wdown_ref[c * FF_CHUNK:(c + 1) * FF_CHUNK, :],
                            preferred_element_type=F32)
        ups = nxt
    out_ref[0] = acc


def _out_ffn(x, part, g1, attnT, w_b_bf, w_o_bf, g_ffn, w_up_bf, conv_f_w, w_down_bf, tm):
    B, T, D = x.shape
    d_ff = w_down_bf.shape[0]
    att = N_HEADS * HEAD_DIM
    assert d_ff % FF_CHUNK == 0
    tok = lambda w: pl.BlockSpec((1, tm, w), lambda b, j: (b, j, 0))
    return pl.pallas_call(
        functools.partial(_out_ffn_kernel, tm, d_ff),
        grid=(B, T // tm),
        in_specs=[
            tok(D), tok(D), tok(D),
            pl.BlockSpec((1, att, tm), lambda b, j: (b, 0, j)),
            _const_spec((att, D)),
            _const_spec((D, D)),
            _const_spec((1, D)),
            _const_spec((D, 2 * d_ff)),
            _const_spec((3, 2 * d_ff)),
            _const_spec((d_ff, D)),
        ],
        out_specs=tok(D),
        out_shape=jax.ShapeDtypeStruct((B, T, D), F32),
        scratch_shapes=[pltpu.VMEM((SUBLANE, 2 * d_ff), F32),
                        pltpu.VMEM((FF_SLOTS * 2 * FF_CHUNK // LANE, tm + SUBLANE, LANE), F32)],
        compiler_params=pltpu.CompilerParams(
            dimension_semantics=("arbitrary", "arbitrary"), vmem_limit_bytes=VMEM_LIMIT),
        name="out_ffn",
    )(x, part, g1, attnT, w_b_bf, w_o_bf, g_ffn, w_up_bf, conv_f_w, w_down_bf)


def _layer(x, mem, g_mix, w_in, conv_a_w, w_a_out, q_norm_g, k_norm_g, w_b_out, g_mem, w_mem_kv,
           mq_norm_g, mk_norm_g, w_m_out, w_o, g_ffn, w_up, conv_f_w, w_down):
    B, T, D = x.shape
    tm_in = min(TM_IN, T)
    tm_out = min(TM_OUT, T)
    k_sel = min(TOPK_MAX, T // 4)
    row = lambda g: g.reshape(1, -1).astype(F32)

    kmT, vm = _mem_kv(mem, row(g_mem), w_mem_kv.astype(BF16), row(mk_norm_g))

    scale = HEAD_DIM ** -0.5
    gk_pad = jnp.pad(row(k_norm_g), ((0, 0), (0, HEAD_PAD - HEAD_DIM)))
    gqT = jnp.broadcast_to((q_norm_g.astype(F32) * scale)[:, None], (HEAD_DIM, tm_in))
    part, g1, qaugT, kaug, vT, qiT, kw, wiT = _in_proj(
        x, row(g_mix), _relayout_w_in(w_in, D), conv_a_w, w_a_out.astype(BF16), gk_pad, gqT,
        kmT, vm, row(mq_norm_g), w_m_out.astype(BF16), tm_in)

    attnT = _dsa(kw, qiT, wiT, kaug, qaugT, vT, k_sel)

    return _out_ffn(x, part, g1, attnT, w_b_out.astype(BF16), w_o.astype(BF16), row(g_ffn),
                    w_up.astype(BF16), conv_f_w, w_down.astype(BF16), tm_out)


def kernel(x, mem, g_mix, w_in, conv_a_w, w_a_out, q_norm_g, k_norm_g, w_b_out, g_mem, w_mem_kv,
           mq_norm_g, mk_norm_g, w_m_out, w_o, g_ffn, w_up, conv_f_w, w_down):
    for l in range(g_mix.shape[0]):
        x = _layer(x, mem, g_mix[l], w_in[l], conv_a_w[l], w_a_out[l], q_norm_g[l], k_norm_g[l],
                   w_b_out[l], g_mem[l], w_mem_kv[l], mq_norm_g[l], mk_norm_g[l], w_m_out[l],
                   w_o[l], g_ffn[l], w_up[l], conv_f_w[l], w_down[l])
    return x
```

```python
import functools

import jax
import jax.numpy as jnp
from jax import lax
from jax.experimental import pallas as pl
from jax.experimental.pallas import tpu as pltpu

F32 = jnp.float32
BF16 = jnp.bfloat16

EPS = 1e-6
N_HEADS = 8
HEAD_DIM = 64
IDX_HEADS = 8
IDX_DIM = 32
MEM_HEADS = 4
MEM_HEAD_DIM = 128
TOPK_MAX = 256
N_BRANCH = 3

LANE = 128
SUBLANE = 8
HEAD_PAD = LANE
POS_SPLIT = 8.0
VMEM_LIMIT = 60 * 1024 * 1024

TM_IN = 512
TM_OUT = 512
QB = 256
KT = 256
FF_CHUNK = 256
FF_SLOTS = 4
BISECT_FIXED = 15
NEG_BIG = -1e30


def _const_spec(shape):
    nd = len(shape)
    return pl.BlockSpec(shape, lambda *_: (0,) * nd, pipeline_mode=pl.Buffered(1))


def _causal_conv3(u, prev8, w3, buf_ref, slot):
    rows = u.shape[0]
    outs = []
    for c in range(u.shape[1] // LANE):
        cs = slice(c * LANE, (c + 1) * LANE)
        buf_ref[slot + c, 0:SUBLANE, :] = prev8[:, cs]
        buf_ref[slot + c, SUBLANE:SUBLANE + rows, :] = u[:, cs]
        u1 = buf_ref[slot + c, pl.ds(SUBLANE - 1, rows), :]
        u2 = buf_ref[slot + c, pl.ds(SUBLANE - 2, rows), :]
        outs.append(w3[0:1, cs] * u2 + w3[1:2, cs] * u1 + w3[2:3, cs] * u[:, cs])
    return jnp.concatenate(outs, axis=1)


def _rms_rows(x, g):
    return x * lax.rsqrt(jnp.mean(x * x, axis=-1, keepdims=True) + EPS) * g


def _mem_kv_kernel(mem_ref, gmem_ref, wkv_ref, gmk_ref, kmT_ref, vm_ref):
    memn = _rms_rows(mem_ref[0], gmem_ref[...]).astype(BF16)
    width = MEM_HEADS * MEM_HEAD_DIM
    km = jnp.dot(memn, wkv_ref[:, 0:width], preferred_element_type=F32)
    vm = jnp.dot(memn, wkv_ref[:, width:2 * width], preferred_element_type=F32)
    heads = []
    for h in range(MEM_HEADS):
        kh = km[:, h * MEM_HEAD_DIM:(h + 1) * MEM_HEAD_DIM]
        heads.append(_rms_rows(kh, gmk_ref[...]))
    kmT_ref[0] = jnp.concatenate(heads, axis=1).T.astype(BF16)
    vm_ref[0] = vm.astype(BF16)


def _mem_kv(mem, g_mem, w_kv_bf, mk_g):
    B, M, D = mem.shape
    width = MEM_HEADS * MEM_HEAD_DIM
    return pl.pallas_call(
        _mem_kv_kernel,
        grid=(B,),
        in_specs=[
            pl.BlockSpec((1, M, D), lambda b: (b, 0, 0)),
            _const_spec((1, D)),
            _const_spec((D, 2 * width)),
            _const_spec((1, MEM_HEAD_DIM)),
        ],
        out_specs=[
            pl.BlockSpec((1, width, M), lambda b: (b, 0, 0)),
            pl.BlockSpec((1, M, width), lambda b: (b, 0, 0)),
        ],
        out_shape=[
            jax.ShapeDtypeStruct((B, width, M), BF16),
            jax.ShapeDtypeStruct((B, M, width), BF16),
        ],
        compiler_params=pltpu.CompilerParams(
            dimension_semantics=("arbitrary",), vmem_limit_bytes=VMEM_LIMIT),
        name="mem_kv",
    )(mem, g_mem, w_kv_bf, mk_g)


class _InCols:
    def __init__(self, d_model):
        sc = d_model // 2
        self.sc = sc
        self.bch = 0
        self.q = self.bch + 3 * sc
        self.k = self.q + N_HEADS * HEAD_DIM
        self.v = self.k + N_HEADS * HEAD_PAD
        self.qi = self.v + N_HEADS * HEAD_DIM
        self.kw = self.qi + IDX_HEADS * IDX_DIM
        self.qm = self.kw + LANE
        self.gates = self.qm + MEM_HEADS * MEM_HEAD_DIM
        self.total = self.gates + N_BRANCH * d_model


def _relayout_w_in(w_in, d_model):
    sc = d_model // 2
    att = N_HEADS * HEAD_DIM
    splits = [sc, sc, sc, att, att, att, IDX_HEADS * IDX_DIM, IDX_DIM, IDX_HEADS,
              MEM_HEADS * MEM_HEAD_DIM, N_BRANCH * d_model]
    offs = [0]
    for s in splits:
        offs.append(offs[-1] + s)
    piece = lambda i: w_in[:, offs[i]:offs[i + 1]]

    def pad_heads(w):
        w = w.reshape(d_model, N_HEADS, HEAD_DIM)
        w = jnp.pad(w, ((0, 0), (0, 0), (0, HEAD_PAD - HEAD_DIM)))
        return w.reshape(d_model, N_HEADS * HEAD_PAD)

    kw = jnp.pad(jnp.concatenate([piece(7), piece(8)], axis=1),
                 ((0, 0), (0, LANE - IDX_DIM - IDX_HEADS)))
    cols = [piece(0), piece(1), piece(2), piece(3), pad_heads(piece(4)), piece(5),
            piece(6), kw, piece(9), piece(10)]
    return jnp.concatenate(cols, axis=1).astype(BF16)


def _in_proj_kernel(cols, tm, d_model,
                    x_ref, gmix_ref, w_ref, convw_ref, waout_ref, gk_ref, gqT_ref,
                    kmT_ref, vm_ref, gmq_ref, wmout_ref,
                    part_ref, g1_ref, qaugT_ref, kaug_ref, vT_ref, qiT_ref, kw_ref, wiT_ref,
                    halo_ref, cbuf_ref):
    j = pl.program_id(1)
    sc = cols.sc
    xn = _rms_rows(x_ref[0], gmix_ref[...]).astype(BF16)

    def proj(off, width):
        return jnp.dot(xn, w_ref[:, off:off + width], preferred_element_type=F32)

    @pl.when(j == 0)
    def _():
        halo_ref[...] = jnp.zeros_like(halo_ref)

    ch = proj(cols.bch + sc, sc) * proj(cols.bch + 2 * sc, sc)
    conv = _causal_conv3(ch, halo_ref[...], convw_ref[...], cbuf_ref, 0)
    halo_ref[...] = ch[tm - SUBLANE:tm, :]
    ua = (proj(cols.bch, sc) * conv).astype(BF16)
    ya = jnp.dot(ua, waout_ref[...], preferred_element_type=F32)

    qm = proj(cols.qm, MEM_HEADS * MEM_HEAD_DIM)
    mem_scale = MEM_HEAD_DIM ** -0.5
    heads = []
    for h in range(MEM_HEADS):
        sl = slice(h * MEM_HEAD_DIM, (h + 1) * MEM_HEAD_DIM)
        qh = _rms_rows(qm[:, sl], gmq_ref[...]).astype(BF16)
        lg = jnp.dot(qh, kmT_ref[0, sl, :], preferred_element_type=F32) * mem_scale
        p = jnp.exp(lg - jnp.max(lg, axis=-1, keepdims=True))
        denom = jnp.sum(p, axis=-1, keepdims=True)
        oh = jnp.dot(p.astype(BF16), vm_ref[0, :, sl], preferred_element_type=F32)
        heads.append(oh / denom)
    om = jnp.concatenate(heads, axis=1).astype(BF16)
    ym = jnp.dot(om, wmout_ref[...], preferred_element_type=F32)

    g0 = jax.nn.sigmoid(proj(cols.gates, d_model))
    g2 = jax.nn.sigmoid(proj(cols.gates + 2 * d_model, d_model))
    part_ref[0] = (g0 * ya + g2 * ym).astype(BF16)
    g1_ref[0] = jax.nn.sigmoid(proj(cols.gates + d_model, d_model)).astype(BF16)

    qT = proj(cols.q, N_HEADS * HEAD_DIM).T
    row8 = lax.broadcasted_iota(jnp.int32, (SUBLANE, tm), 0)
    zpad = jnp.zeros((HEAD_PAD - HEAD_DIM - SUBLANE, tm), F32)
    for h in range(N_HEADS):
        blk = qT[h * HEAD_DIM:(h + 1) * HEAD_DIM, :]
        r = lax.rsqrt(jnp.sum(blk * blk, axis=0, keepdims=True) * (1.0 / HEAD_DIM) + EPS)
        slope = 2.0 ** (-8.0 * (h + 1) / N_HEADS)
        aug = jnp.where(row8 < 2, slope, 0.0).astype(F32)
        full = jnp.concatenate([blk * r * gqT_ref[...], aug, zpad], axis=0)
        qaugT_ref[0, h * HEAD_PAD:(h + 1) * HEAD_PAD, :] = full.astype(BF16)

    kraw = proj(cols.k, N_HEADS * HEAD_PAD)
    posf = (j * tm + lax.broadcasted_iota(jnp.int32, (tm, HEAD_PAD), 0)).astype(F32)
    lane = lax.broadcasted_iota(jnp.int32, (tm, HEAD_PAD), 1)
    pos_hi = jnp.floor(posf * (1.0 / POS_SPLIT)) * POS_SPLIT
    posmat = jnp.where(lane == HEAD_DIM, pos_hi, jnp.where(lane == HEAD_DIM + 1, posf - pos_hi, 0.0))
    for h in range(N_HEADS):
        kg = kraw[:, h * HEAD_PAD:(h + 1) * HEAD_PAD]
        r = lax.rsqrt(jnp.sum(kg * kg, axis=-1, keepdims=True) * (1.0 / HEAD_DIM) + EPS)
        kaug_ref[0, :, h * HEAD_PAD:(h + 1) * HEAD_PAD] = (kg * r * gk_ref[...] + posmat).astype(BF16)

    vT_ref[0] = proj(cols.v, N_HEADS * HEAD_DIM).T.astype(BF16)
    qiT_ref[0] = proj(cols.qi, IDX_HEADS * IDX_DIM).T.astype(BF16)
    kw = proj(cols.kw, LANE)
    kw_ref[0] = kw.astype(BF16)
    wiT_ref[0] = kw.T[IDX_DIM:IDX_DIM + IDX_HEADS, :]


def _in_proj(x, g_mix, w_perm, conv_a_w, w_a_out_bf, gk_pad, gqT, kmT, vm, mq_g, w_m_out_bf, tm):
    B, T, D = x.shape
    cols = _InCols(D)
    M = vm.shape[1]
    memw = MEM_HEADS * MEM_HEAD_DIM
    tok = lambda w: pl.BlockSpec((1, tm, w), lambda b, j: (b, j, 0))
    tokT = lambda w: pl.BlockSpec((1, w, tm), lambda b, j: (b, 0, j))
    return pl.pallas_call(
        functools.partial(_in_proj_kernel, cols, tm, D),
        grid=(B, T // tm),
        in_specs=[
            tok(D),
            _const_spec((1, D)),
            _const_spec((D, cols.total)),
            _const_spec((3, cols.sc)),
            _const_spec((cols.sc, D)),
            _const_spec((1, HEAD_PAD)),
            _const_spec((HEAD_DIM, tm)),
            pl.BlockSpec((1, memw, M), lambda b, j: (b, 0, 0)),
            pl.BlockSpec((1, M, memw), lambda b, j: (b, 0, 0)),
            _const_spec((1, MEM_HEAD_DIM)),
            _const_spec((memw, D)),
        ],
        out_specs=[
            tok(D), tok(D),
            tokT(N_HEADS * HEAD_PAD), tok(N_HEADS * HEAD_PAD),
            tokT(N_HEADS * HEAD_DIM), tokT(IDX_HEADS * IDX_DIM),
            tok(LANE), tokT(IDX_HEADS),
        ],
        out_shape=[
            jax.ShapeDtypeStruct((B, T, D), BF16),
            jax.ShapeDtypeStruct((B, T, D), BF16),
            jax.ShapeDtypeStruct((B, N_HEADS * HEAD_PAD, T), BF16),
            jax.ShapeDtypeStruct((B, T, N_HEADS * HEAD_PAD), BF16),
            jax.ShapeDtypeStruct((B, N_HEADS * HEAD_DIM, T), BF16),
            jax.ShapeDtypeStruct((B, IDX_HEADS * IDX_DIM, T), BF16),
            jax.ShapeDtypeStruct((B, T, LANE), BF16),
            jax.ShapeDtypeStruct((B, IDX_HEADS, T), F32),
        ],
        scratch_shapes=[pltpu.VMEM((SUBLANE, cols.sc), F32),
                        pltpu.VMEM((cols.sc // LANE, tm + SUBLANE, LANE), F32)],
        compiler_params=pltpu.CompilerParams(
            dimension_semantics=("arbitrary", "arbitrary"), vmem_limit_bytes=VMEM_LIMIT),
        name="in_proj",
    )(x, g_mix, w_perm, conv_a_w, w_a_out_bf, gk_pad, gqT, kmT, vm, mq_g, w_m_out_bf)


def _dsa_kernel(k_sel, kw_ref, qiT_ref, wiT_ref, kaug_ref, qaugT_ref, vT_ref, out_ref,
                s_ref, lg_ref, acc_ref):
    i = pl.program_id(1)
    n_tiles = i + 1
    qpos = i * QB + lax.broadcasted_iota(jnp.int32, (1, QB), 1)
    row = lax.broadcasted_iota(jnp.int32, (KT, QB), 0)
    groups = KT // SUBLANE
    inf = jnp.float32(jnp.inf)

    def tile_start(t):
        return pl.multiple_of(t * KT, KT)

    def for_tiles(body, init):
        carry = lax.fori_loop(0, n_tiles // 2, lambda p, c: body(2 * p + 1, body(2 * p, c)), init)
        return lax.cond(n_tiles % 2 == 1, lambda c: body(n_tiles - 1, c), lambda c: c, carry)

    zrows = jnp.zeros((LANE - IDX_DIM, QB), BF16)

    def score_tile(t, carry):
        lo_part, hi_part = carry
        r0 = tile_start(t)
        kw_t = kw_ref[0, pl.ds(r0, KT), :]
        acc = jnp.zeros((KT, QB), F32)
        for h in range(IDX_HEADS):
            rhs = jnp.concatenate([qiT_ref[0, h * IDX_DIM:(h + 1) * IDX_DIM, :], zrows], axis=0)
            lg = jnp.dot(kw_t, rhs, preferred_element_type=F32)
            acc = acc + wiT_ref[0, h:h + 1, :] * jnp.maximum(lg, 0.0)
        valid = r0 + row <= qpos
        s_ref[pl.ds(r0, KT), :] = jnp.where(valid, acc, -inf)
        lo_t = jnp.min(jnp.where(valid, acc, inf).reshape(groups, SUBLANE, QB), axis=0)
        hi_t = jnp.max(jnp.where(valid, acc, -inf).reshape(groups, SUBLANE, QB), axis=0)
        return jnp.minimum(lo_part, lo_t), jnp.maximum(hi_part, hi_t)

    lo_part, hi_part = for_tiles(
        score_tile,
        (jnp.full((SUBLANE, QB), jnp.inf, F32), jnp.full((SUBLANE, QB), -jnp.inf, F32)))

    def tree(x3, op):
        while x3.shape[0] > 1:
            half = x3.shape[0] // 2
            x3 = op(x3[:half], x3[half:])
        return x3[0]

    def load3(t):
        return s_ref[pl.ds(tile_start(t), KT), :].reshape(groups, SUBLANE, QB)

    def bcast8(v):
        return jnp.broadcast_to(v, (SUBLANE, QB))[None]

    def count_ge(v):
        v8 = bcast8(v)

        def body(t, part):
            return part + tree(jnp.where(load3(t) >= v8, 1.0, 0.0), jnp.add)

        part = for_tiles(body, jnp.zeros((SUBLANE, QB), F32))
        return jnp.sum(part, axis=0, keepdims=True)

    def bracket_ends(lo, hi):
        lo8, hi8 = bcast8(lo), bcast8(hi)

        def body(t, carry):
            s3 = load3(t)
            a_t = tree(jnp.where(s3 >= lo8, s3, inf), jnp.minimum)
            b_t = tree(jnp.where(s3 < hi8, s3, -inf), jnp.maximum)
            return jnp.minimum(carry[0], a_t), jnp.maximum(carry[1], b_t)

        a_part, b_part = lax.fori_loop(
            0, n_tiles, body,
            (jnp.full((SUBLANE, QB), jnp.inf, F32), jnp.full((SUBLANE, QB), -jnp.inf, F32)))
        return jnp.min(a_part, axis=0, keepdims=True), jnp.max(b_part, axis=0, keepdims=True)

    kf = jnp.minimum(qpos + 1, k_sel).astype(F32)
    top = jnp.max(hi_part, axis=0, keepdims=True)
    lo0 = jnp.min(lo_part, axis=0, keepdims=True)
    hi0 = top + (top - lo0) + jnp.maximum(jnp.abs(top), 1.0) * 1e-3
    clo0 = (qpos + 1).astype(F32)

    def split(st, mid):
        lo, hi, clo, chi = st
        c = count_ge(mid)
        ge = c >= kf
        return (jnp.where(ge, mid, lo), jnp.where(ge, hi, mid),
                jnp.where(ge, c, clo), jnp.where(ge, chi, c))

    def bisect(_, st):
        lo, hi = st[0], st[1]
        mid = lo + (hi - lo) * 0.5
        return split(st, jnp.where(mid <= lo, hi, mid))

    st = lax.fori_loop(0, BISECT_FIXED, bisect, (lo0, hi0, clo0, jnp.zeros((1, QB), F32)))

    def unresolved(a, b):
        return jnp.max(jnp.where(a < b, 1.0, 0.0))

    def refine(carry):
        st, a, b = carry[:4], carry[4], carry[5]
        mid = a + (b - a) * 0.5
        st = split(st, jnp.where(mid <= a, b, mid))
        a, b = bracket_ends(st[0], st[1])
        return st + (a, b, unresolved(a, b))

    a0, b0 = bracket_ends(st[0], st[1])
    lo, hi, clo, cgt, tau, _, _ = lax.while_loop(
        lambda c: c[6] > 0.0, refine, st + (a0, b0, unresolved(a0, b0)))

    need = kf - cgt

    @pl.when(jnp.max(jnp.where(clo > kf, 1.0, 0.0)) > 0.0)
    def _():
        tri = jnp.where(row >= lax.broadcasted_iota(jnp.int32, (KT, KT), 1), 1.0, 0.0).astype(BF16)

        def drop_tile(t, before):
            r0 = tile_start(t)
            s_t = s_ref[pl.ds(r0, KT), :]
            tied = jnp.where(s_t == tau, 1.0, 0.0)
            rank = before + jnp.dot(tri, tied.astype(BF16), preferred_element_type=F32)
            s_ref[pl.ds(r0, KT), :] = jnp.where(tied * rank > need, -inf, s_t)
            return before + jnp.sum(tied, axis=0, keepdims=True)

        lax.fori_loop(0, n_tiles, drop_tile, jnp.zeros((1, QB), F32))

    def logits_tile(t, mparts):
        r0 = tile_start(t)
        bias = jnp.where(s_ref[pl.ds(r0, KT), :] >= tau, 0.0, -inf)
        out = []
        for h in range(N_HEADS):
            hs = slice(h * HEAD_PAD, (h + 1) * HEAD_PAD)
            s = jnp.dot(kaug_ref[0, pl.ds(r0, KT), hs], qaugT_ref[0, hs, :],
                        preferred_element_type=F32) + bias
            lg_ref[h, pl.ds(r0, KT), :] = s
            out.append(jnp.maximum(mparts[h], jnp.max(s.reshape(groups, SUBLANE, QB), axis=0)))
        return tuple(out)

    mparts = for_tiles(logits_tile,
                       tuple(jnp.full((SUBLANE, QB), NEG_BIG, F32) for _ in range(N_HEADS)))
    m_rows = [jnp.max(mp, axis=0, keepdims=True) for mp in mparts]

    acc_ref[...] = jnp.zeros(acc_ref.shape, F32)

    def pv_tile(t, lparts):
        r0 = tile_start(t)
        out = []
        for h in range(N_HEADS):
            vs = slice(h * HEAD_DIM, (h + 1) * HEAD_DIM)
            p = jnp.exp(lg_ref[h, pl.ds(r0, KT), :] - m_rows[h])
            out.append(lparts[h] + jnp.sum(p.reshape(groups, SUBLANE, QB), axis=0))
            acc_ref[vs, :] += jnp.dot(vT_ref[0, vs, pl.ds(r0, KT)], p.astype(BF16),
                                      preferred_element_type=F32)
        return tuple(out)

    lparts = lax.fori_loop(0, n_tiles, pv_tile,
                           tuple(jnp.zeros((SUBLANE, QB), F32) for _ in range(N_HEADS)))

    for h in range(N_HEADS):
        vs = slice(h * HEAD_DIM, (h + 1) * HEAD_DIM)
        denom = jnp.sum(lparts[h], axis=0, keepdims=True)
        out_ref[0, vs, :] = (acc_ref[vs, :] / denom).astype(BF16)


def _dsa(kw, qiT, wiT, kaug, qaugT, vT, k_sel):
    B, T, _ = kaug.shape
    assert T % QB == 0 and QB == KT
    att = N_HEADS * HEAD_DIM
    qblk = lambda w: pl.BlockSpec((1, w, QB), lambda b, i: (b, 0, i))
    return pl.pallas_call(
        functools.partial(_dsa_kernel, k_sel),
        grid=(B, T // QB),
        in_specs=[
            pl.BlockSpec((1, T, LANE), lambda b, i: (b, 0, 0)),
            qblk(IDX_HEADS * IDX_DIM),
            qblk(IDX_HEADS),
            pl.BlockSpec((1, T, N_HEADS * HEAD_PAD), lambda b, i: (b, 0, 0)),
            qblk(N_HEADS * HEAD_PAD),
            pl.BlockSpec((1, att, T), lambda b, i: (b, 0, 0)),
        ],
        out_specs=qblk(att),
        out_shape=jax.ShapeDtypeStruct((B, att, T), BF16),
        scratch_shapes=[
            pltpu.VMEM((T, QB), F32),
            pltpu.VMEM((N_HEADS, T, QB), F32),
            pltpu.VMEM((att, QB), F32),
        ],
        compiler_params=pltpu.CompilerParams(
            dimension_semantics=("arbitrary", "arbitrary"), vmem_limit_bytes=VMEM_LIMIT),
        name="dsa",
    )(kw, qiT, wiT, kaug, qaugT, vT)


def _out_ffn_kernel(tm, d_ff, x_ref, part_ref, g1_ref, attnT_ref, wb_ref, wo_ref, gffn_ref,
                    wup_ref, convf_ref, wdown_ref, out_ref, halo_ref, cbuf_ref):
    j = pl.program_id(1)
    yb = lax.dot_general(attnT_ref[0], wb_ref[...], (((0,), (0,)), ((), ())),
                         preferred_element_type=F32)
    merged = (part_ref[0].astype(F32) + g1_ref[0].astype(F32) * yb).astype(BF16)
    x1 = x_ref[0] + jnp.dot(merged, wo_ref[...], preferred_element_type=F32)
    xn2 = _rms_rows(x1, gffn_ref[...]).astype(BF16)

    @pl.when(j == 0)
    def _():
        halo_ref[...] = jnp.zeros_like(halo_ref)

    def up_proj(c):
        return tuple(jnp.dot(xn2, wup_ref[:, off:off + FF_CHUNK], preferred_element_type=F32)
                     for off in (c * FF_CHUNK, d_ff + c * FF_CHUNK))

    def conv_chunk(up, off, slot):
        cs = slice(off, off + FF_CHUNK)
        out = _causal_conv3(up, halo_ref[:, cs], convf_ref[:, cs], cbuf_ref, slot)
        halo_ref[:, cs] = up[tm - SUBLANE:tm, :]
        return out

    n_chunks = d_ff // FF_CHUNK
    blocks = FF_CHUNK // LANE
    acc = x1
    ups = up_proj(0)
    for c in range(n_chunks):
        nxt = up_proj(c + 1) if c + 1 < n_chunks else None
        slot = (c % FF_SLOTS) * 2 * blocks
        gate = conv_chunk(ups[0], c * FF_CHUNK, slot)
        val = conv_chunk(ups[1], d_ff + c * FF_CHUNK, slot + blocks)
        hmid = (gate * jax.nn.sigmoid(gate) * val).astype(BF16)
        acc = acc + jnp.dot(hmid, wdown_ref[c * FF_CHUNK:(c + 1) * FF_CHUNK, :],
                            preferred_element_type=F32)
        ups = nxt
    out_ref[0] = acc


def _out_ffn(x, part, g1, attnT, w_b_bf, w_o_bf, g_ffn, w_up_bf, conv_f_w, w_down_bf, tm):
    B, T, D = x.shape
    d_ff = w_down_bf.shape[0]
    att = N_HEADS * HEAD_DIM
    assert d_ff % FF_CHUNK == 0
    tok = lambda w: pl.BlockSpec((1, tm, w), lambda b, j: (b, j, 0))
    return pl.pallas_call(
        functools.partial(_out_ffn_kernel, tm, d_ff),
        grid=(B, T // tm),
        in_specs=[
            tok(D), tok(D), tok(D),
            pl.BlockSpec((1, att, tm), lambda b, j: (b, 0, j)),
            _const_spec((att, D)),
            _const_spec((D, D)),
            _const_spec((1, D)),
            _const_spec((D, 2 * d_ff)),
            _const_spec((3, 2 * d_ff)),
            _const_spec((d_ff, D)),
        ],
        out_specs=tok(D),
        out_shape=jax.ShapeDtypeStruct((B, T, D), F32),
        scratch_shapes=[pltpu.VMEM((SUBLANE, 2 * d_ff), F32),
                        pltpu.VMEM((FF_SLOTS * 2 * FF_CHUNK // LANE, tm + SUBLANE, LANE), F32)],
        compiler_params=pltpu.CompilerParams(
            dimension_semantics=("arbitrary", "arbitrary"), vmem_limit_bytes=VMEM_LIMIT),
        name="out_ffn",
    )(x, part, g1, attnT, w_b_bf, w_o_bf, g_ffn, w_up_bf, conv_f_w, w_down_bf)


def _layer(x, mem, g_mix, w_in, conv_a_w, w_a_out, q_norm_g, k_norm_g, w_b_out, g_mem, w_mem_kv,
           mq_norm_g, mk_norm_g, w_m_out, w_o, g_ffn, w_up, conv_f_w, w_down):
    B, T, D = x.shape
    tm_in = min(TM_IN, T)
    tm_out = min(TM_OUT, T)
    k_sel = min(TOPK_MAX, T // 4)
    row = lambda g: g.reshape(1, -1).astype(F32)

    kmT, vm = _mem_kv(mem, row(g_mem), w_mem_kv.astype(BF16), row(mk_norm_g))

    scale = HEAD_DIM ** -0.5
    gk_pad = jnp.pad(row(k_norm_g), ((0, 0), (0, HEAD_PAD - HEAD_DIM)))
    gqT = jnp.broadcast_to((q_norm_g.astype(F32) * scale)[:, None], (HEAD_DIM, tm_in))
    part, g1, qaugT, kaug, vT, qiT, kw, wiT = _in_proj(
        x, row(g_mix), _relayout_w_in(w_in, D), conv_a_w, w_a_out.astype(BF16), gk_pad, gqT,
        kmT, vm, row(mq_norm_g), w_m_out.astype(BF16), tm_in)

    attnT = _dsa(kw, qiT, wiT, kaug, qaugT, vT, k_sel)

    return _out_ffn(x, part, g1, attnT, w_b_out.astype(BF16), w_o.astype(BF16), row(g_ffn),
                    w_up.astype(BF16), conv_f_w, w_down.astype(BF16), tm_out)


def kernel(x, mem, g_mix, w_in, conv_a_w, w_a_out, q_norm_g, k_norm_g, w_b_out, g_mem, w_mem_kv,
           mq_norm_g, mk_norm_g, w_m_out, w_o, g_ffn, w_up, conv_f_w, w_down):
    for l in range(g_mix.shape[0]):
        x = _layer(x, mem, g_mix[l], w_in[l], conv_a_w[l], w_a_out[l], q_norm_g[l], k_norm_g[l],
                   w_b_out[l], g_mem[l], w_mem_kv[l], mq_norm_g[l], mk_norm_g[l], w_m_out[l],
                   w_o[l], g_ffn[l], w_up[l], conv_f_w[l], w_down[l])
    return x
```

```python
import functools

import jax
import jax.numpy as jnp
import numpy as np
from jax import lax
from jax.experimental import pallas as pl
from jax.experimental.pallas import tpu as pltpu

F32 = jnp.float32
BF16 = jnp.bfloat16

EPS = 1e-6
N_HEADS = 8
HEAD_DIM = 64
IDX_HEADS = 8
IDX_DIM = 32
MEM_HEADS = 4
MEM_HEAD_DIM = 128
TOPK_MAX = 256
N_BRANCH = 3

LANE = 128
SUBLANE = 8
HEAD_PAD = LANE
POS_SPLIT = 8.0
LOG2E = 1.4426950408889634
SLOPE_TERMS = 3
VMEM_LIMIT = 60 * 1024 * 1024

TM_IN = 512
TM_OUT = 512
QB = 256
KT = 256
FF_CHUNK = 256
FF_SLOTS = 4
VROWS = HEAD_DIM + 16
BISECT_FIXED = 15
NEG_BIG = -1e30


def _const_spec(shape):
    nd = len(shape)
    return pl.BlockSpec(shape, lambda *_: (0,) * nd, pipeline_mode=pl.Buffered(1))


def _causal_conv3(u, prev8, w3, buf_ref, slot):
    rows = u.shape[0]
    outs = []
    for c in range(u.shape[1] // LANE):
        cs = slice(c * LANE, (c + 1) * LANE)
        buf_ref[slot + c, 0:SUBLANE, :] = prev8[:, cs]
        buf_ref[slot + c, SUBLANE:SUBLANE + rows, :] = u[:, cs]
        u1 = buf_ref[slot + c, pl.ds(SUBLANE - 1, rows), :]
        u2 = buf_ref[slot + c, pl.ds(SUBLANE - 2, rows), :]
        outs.append(w3[0:1, cs] * u2 + w3[1:2, cs] * u1 + w3[2:3, cs] * u[:, cs])
    return jnp.concatenate(outs, axis=1)


def _bf16_terms(c):
    rest, out = np.float32(c), []
    for _ in range(SLOPE_TERMS):
        term = np.float32(np.asarray(rest, dtype=jnp.bfloat16))
        out.append(float(term))
        rest = np.float32(rest - term)
    return out


def _rms_rows(x, g):
    return x * lax.rsqrt(jnp.mean(x * x, axis=-1, keepdims=True) + EPS) * g


def _mem_kv_kernel(mem_ref, gmem_ref, wkv_ref, gmk_ref, kmT_ref, vm_ref):
    memn = _rms_rows(mem_ref[0], gmem_ref[...]).astype(BF16)
    width = MEM_HEADS * MEM_HEAD_DIM
    km = jnp.dot(memn, wkv_ref[:, 0:width], preferred_element_type=F32)
    vm = jnp.dot(memn, wkv_ref[:, width:2 * width], preferred_element_type=F32)
    heads = []
    for h in range(MEM_HEADS):
        kh = km[:, h * MEM_HEAD_DIM:(h + 1) * MEM_HEAD_DIM]
        heads.append(_rms_rows(kh, gmk_ref[...]))
    kmT_ref[0] = jnp.concatenate(heads, axis=1).T.astype(BF16)
    vm_ref[0] = vm.astype(BF16)


def _mem_kv(mem, g_mem, w_kv_bf, mk_g):
    B, M, D = mem.shape
    width = MEM_HEADS * MEM_HEAD_DIM
    return pl.pallas_call(
        _mem_kv_kernel,
        grid=(B,),
        in_specs=[
            pl.BlockSpec((1, M, D), lambda b: (b, 0, 0)),
            _const_spec((1, D)),
            _const_spec((D, 2 * width)),
            _const_spec((1, MEM_HEAD_DIM)),
        ],
        out_specs=[
            pl.BlockSpec((1, width, M), lambda b: (b, 0, 0)),
            pl.BlockSpec((1, M, width), lambda b: (b, 0, 0)),
        ],
        out_shape=[
            jax.ShapeDtypeStruct((B, width, M), BF16),
            jax.ShapeDtypeStruct((B, M, width), BF16),
        ],
        compiler_params=pltpu.CompilerParams(
            dimension_semantics=("arbitrary",), vmem_limit_bytes=VMEM_LIMIT),
        name="mem_kv",
    )(mem, g_mem, w_kv_bf, mk_g)


class _InCols:
    def __init__(self, d_model):
        sc = d_model // 2
        self.sc = sc
        self.bch = 0
        self.q = self.bch + 3 * sc
        self.k = self.q + N_HEADS * HEAD_DIM
        self.v = self.k + N_HEADS * HEAD_DIM
        self.qi = self.v + N_HEADS * HEAD_DIM
        self.kw = self.qi + IDX_HEADS * IDX_DIM
        self.qm = self.kw + LANE
        self.gates = self.qm + MEM_HEADS * MEM_HEAD_DIM
        self.total = self.gates + N_BRANCH * d_model


def _relayout_w_in(w_in, d_model):
    sc = d_model // 2
    att = N_HEADS * HEAD_DIM
    splits = [sc, sc, sc, att, att, att, IDX_HEADS * IDX_DIM, IDX_DIM, IDX_HEADS,
              MEM_HEADS * MEM_HEAD_DIM, N_BRANCH * d_model]
    offs = [0]
    for s in splits:
        offs.append(offs[-1] + s)
    piece = lambda i: w_in[:, offs[i]:offs[i + 1]]

    kw = jnp.pad(jnp.concatenate([piece(7), piece(8)], axis=1),
                 ((0, 0), (0, LANE - IDX_DIM - IDX_HEADS)))
    cols = [piece(0), piece(1), piece(2), piece(3), piece(4), piece(5),
            piece(6), kw, piece(9), piece(10)]
    return jnp.concatenate(cols, axis=1).astype(BF16)


def _in_proj_kernel(cols, tm, d_model,
                    x_ref, gmix_ref, w_ref, convw_ref, waout_ref, gk_ref, gqT_ref,
                    kmT_ref, vm_ref, gmq_ref, wmout_ref,
                    part_ref, g1_ref, qaugT_ref, kaug_ref, vT_ref, qiT_ref, kw_ref, wiT_ref,
                    halo_ref, cbuf_ref):
    j = pl.program_id(1)
    sc = cols.sc
    xn = _rms_rows(x_ref[0], gmix_ref[...]).astype(BF16)

    def proj(off, width):
        return jnp.dot(xn, w_ref[:, off:off + width], preferred_element_type=F32)

    @pl.when(j == 0)
    def _():
        halo_ref[...] = jnp.zeros_like(halo_ref)

    ch = proj(cols.bch + sc, sc) * proj(cols.bch + 2 * sc, sc)
    conv = _causal_conv3(ch, halo_ref[...], convw_ref[...], cbuf_ref, 0)
    halo_ref[...] = ch[tm - SUBLANE:tm, :]
    ua = (proj(cols.bch, sc) * conv).astype(BF16)
    ya = jnp.dot(ua, waout_ref[...], preferred_element_type=F32)

    qT = proj(cols.q, N_HEADS * HEAD_DIM).T
    row8 = lax.broadcasted_iota(jnp.int32, (SUBLANE, tm), 0)
    zpad = jnp.zeros((HEAD_PAD - HEAD_DIM - SUBLANE, tm), F32)
    for h in range(N_HEADS):
        blk = qT[h * HEAD_DIM:(h + 1) * HEAD_DIM, :]
        r = lax.rsqrt(jnp.sum(blk * blk, axis=0, keepdims=True) * (1.0 / HEAD_DIM) + EPS)
        aug = jnp.zeros((SUBLANE, tm), F32)
        for i, term in enumerate(_bf16_terms(2.0 ** (-8.0 * (h + 1) / N_HEADS) * LOG2E)):
            aug = jnp.where((row8 == i) | (row8 == i + SLOPE_TERMS), term, aug)
        full = jnp.concatenate([blk * r * gqT_ref[...], aug, zpad], axis=0)
        qaugT_ref[0, h * HEAD_PAD:(h + 1) * HEAD_PAD, :] = full.astype(BF16)

    kraw = proj(cols.k, N_HEADS * HEAD_DIM)
    posf = (j * tm + lax.broadcasted_iota(jnp.int32, (tm, HEAD_PAD), 0)).astype(F32)
    lane = lax.broadcasted_iota(jnp.int32, (tm, HEAD_PAD), 1)
    pos_hi = jnp.floor(posf * (1.0 / POS_SPLIT)) * POS_SPLIT
    in_hi = (lane >= HEAD_DIM) & (lane < HEAD_DIM + SLOPE_TERMS)
    in_lo = (lane >= HEAD_DIM + SLOPE_TERMS) & (lane < HEAD_DIM + 2 * SLOPE_TERMS)
    posmat = jnp.where(in_hi, pos_hi, jnp.where(in_lo, posf - pos_hi, 0.0))
    for h in range(N_HEADS):
        pair = kraw[:, (h // 2) * LANE:(h // 2 + 1) * LANE]
        if h % 2:
            pair = pltpu.roll(pair, HEAD_DIM, axis=1)
        kg = jnp.where(lane < HEAD_DIM, pair, 0.0)
        r = lax.rsqrt(jnp.sum(kg * kg, axis=-1, keepdims=True) * (1.0 / HEAD_DIM) + EPS)
        kaug_ref[0, :, h * HEAD_PAD:(h + 1) * HEAD_PAD] = (kg * r * gk_ref[...] + posmat).astype(BF16)

    vT_ref[0] = proj(cols.v, N_HEADS * HEAD_DIM).T.astype(BF16)
    qiT_ref[0] = proj(cols.qi, IDX_HEADS * IDX_DIM).T.astype(BF16)
    kw = proj(cols.kw, LANE)
    kw_ref[0] = kw.astype(BF16)
    wiT_ref[0] = kw.T[IDX_DIM:IDX_DIM + IDX_HEADS, :]

    qm = proj(cols.qm, MEM_HEADS * MEM_HEAD_DIM)
    mem_scale = MEM_HEAD_DIM ** -0.5
    heads = []
    for h in range(MEM_HEADS):
        sl = slice(h * MEM_HEAD_DIM, (h + 1) * MEM_HEAD_DIM)
        qh = _rms_rows(qm[:, sl], gmq_ref[...]).astype(BF16)
        lg = jnp.dot(qh, kmT_ref[0, sl, :], preferred_element_type=F32) * mem_scale
        p = jnp.exp(lg - jnp.max(lg, axis=-1, keepdims=True))
        denom = jnp.sum(p, axis=-1, keepdims=True)
        oh = jnp.dot(p.astype(BF16), vm_ref[0, :, sl], preferred_element_type=F32)
        heads.append(oh / denom)
    om = jnp.concatenate(heads, axis=1).astype(BF16)
    ym = jnp.dot(om, wmout_ref[...], preferred_element_type=F32)

    g0 = jax.nn.sigmoid(proj(cols.gates, d_model))
    g2 = jax.nn.sigmoid(proj(cols.gates + 2 * d_model, d_model))
    part_ref[0] = (g0 * ya + g2 * ym).astype(BF16)
    g1_ref[0] = jax.nn.sigmoid(proj(cols.gates + d_model, d_model)).astype(BF16)


def _in_proj(x, g_mix, w_perm, conv_a_w, w_a_out_bf, gk_pad, gqT, kmT, vm, mq_g, w_m_out_bf, tm):
    B, T, D = x.shape
    cols = _InCols(D)
    M = vm.shape[1]
    memw = MEM_HEADS * MEM_HEAD_DIM
    tok = lambda w: pl.BlockSpec((1, tm, w), lambda b, j: (b, j, 0))
    tokT = lambda w: pl.BlockSpec((1, w, tm), lambda b, j: (b, 0, j))
    return pl.pallas_call(
        functools.partial(_in_proj_kernel, cols, tm, D),
        grid=(B, T // tm),
        in_specs=[
            tok(D),
            _const_spec((1, D)),
            _const_spec((D, cols.total)),
            _const_spec((3, cols.sc)),
            _const_spec((cols.sc, D)),
            _const_spec((1, HEAD_PAD)),
            _const_spec((HEAD_DIM, tm)),
            pl.BlockSpec((1, memw, M), lambda b, j: (b, 0, 0)),
            pl.BlockSpec((1, M, memw), lambda b, j: (b, 0, 0)),
            _const_spec((1, MEM_HEAD_DIM)),
            _const_spec((memw, D)),
        ],
        out_specs=[
            tok(D), tok(D),
            tokT(N_HEADS * HEAD_PAD), tok(N_HEADS * HEAD_PAD),
            tokT(N_HEADS * HEAD_DIM), tokT(IDX_HEADS * IDX_DIM),
            tok(LANE), tokT(IDX_HEADS),
        ],
        out_shape=[
            jax.ShapeDtypeStruct((B, T, D), BF16),
            jax.ShapeDtypeStruct((B, T, D), BF16),
            jax.ShapeDtypeStruct((B, N_HEADS * HEAD_PAD, T), BF16),
            jax.ShapeDtypeStruct((B, T, N_HEADS * HEAD_PAD), BF16),
            jax.ShapeDtypeStruct((B, N_HEADS * HEAD_DIM, T), BF16),
            jax.ShapeDtypeStruct((B, IDX_HEADS * IDX_DIM, T), BF16),
            jax.ShapeDtypeStruct((B, T, LANE), BF16),
            jax.ShapeDtypeStruct((B, IDX_HEADS, T), F32),
        ],
        scratch_shapes=[pltpu.VMEM((SUBLANE, cols.sc), F32),
                        pltpu.VMEM((cols.sc // LANE, tm + SUBLANE, LANE), F32)],
        compiler_params=pltpu.CompilerParams(
            dimension_semantics=("arbitrary", "arbitrary"), vmem_limit_bytes=VMEM_LIMIT),
        name="in_proj",
    )(x, g_mix, w_perm, conv_a_w, w_a_out_bf, gk_pad, gqT, kmT, vm, mq_g, w_m_out_bf)


def _dsa_kernel(k_sel, kw_ref, qiT_ref, wiT_ref, kaug_ref, qaugT_ref, vT_ref, out_ref,
                s_ref, lg_ref, acc_ref):
    i = pl.program_id(1)
    n_tiles = i + 1
    qpos = i * QB + lax.broadcasted_iota(jnp.int32, (1, QB), 1)
    row = lax.broadcasted_iota(jnp.int32, (KT, QB), 0)
    groups = KT // SUBLANE
    inf = jnp.float32(jnp.inf)

    def tile_start(t):
        return pl.multiple_of(t * KT, KT)

    def for_tiles(body, init):
        carry = lax.fori_loop(0, n_tiles // 2, lambda p, c: body(2 * p + 1, body(2 * p, c)), init)
        return lax.cond(n_tiles % 2 == 1, lambda c: body(n_tiles - 1, c), lambda c: c, carry)

    zrows = jnp.zeros((LANE - IDX_DIM, QB), BF16)

    def score_tile(t, carry):
        lo_part, hi_part = carry
        r0 = tile_start(t)
        kw_t = kw_ref[0, pl.ds(r0, KT), :]
        acc = jnp.zeros((KT, QB), F32)
        for h in range(IDX_HEADS):
            rhs = jnp.concatenate([qiT_ref[0, h * IDX_DIM:(h + 1) * IDX_DIM, :], zrows], axis=0)
            lg = jnp.dot(kw_t, rhs, preferred_element_type=F32)
            acc = acc + wiT_ref[0, h:h + 1, :] * jnp.maximum(lg, 0.0)
        valid = r0 + row <= qpos
        s_ref[pl.ds(r0, KT), :] = jnp.where(valid, acc, -inf)
        lo_t = jnp.min(jnp.where(valid, acc, inf).reshape(groups, SUBLANE, QB), axis=0)
        hi_t = jnp.max(jnp.where(valid, acc, -inf).reshape(groups, SUBLANE, QB), axis=0)
        return jnp.minimum(lo_part, lo_t), jnp.maximum(hi_part, hi_t)

    lo_part, hi_part = for_tiles(
        score_tile,
        (jnp.full((SUBLANE, QB), jnp.inf, F32), jnp.full((SUBLANE, QB), -jnp.inf, F32)))

    def tree(x3, op):
        while x3.shape[0] > 1:
            half = x3.shape[0] // 2
            x3 = op(x3[:half], x3[half:])
        return x3[0]

    def load3(t):
        return s_ref[pl.ds(tile_start(t), KT), :].reshape(groups, SUBLANE, QB)

    def bcast8(v):
        return jnp.broadcast_to(v, (SUBLANE, QB))[None]

    def count_ge(v):
        v8 = bcast8(v)

        def body(t, part):
            return part + tree(jnp.where(load3(t) >= v8, 1.0, 0.0), jnp.add)

        part = for_tiles(body, jnp.zeros((SUBLANE, QB), F32))
        return jnp.sum(part, axis=0, keepdims=True)

    def bracket_ends(lo, hi):
        lo8, hi8 = bcast8(lo), bcast8(hi)

        def body(t, carry):
            s3 = load3(t)
            a_t = tree(jnp.where(s3 >= lo8, s3, inf), jnp.minimum)
            b_t = tree(jnp.where(s3 < hi8, s3, -inf), jnp.maximum)
            return jnp.minimum(carry[0], a_t), jnp.maximum(carry[1], b_t)

        a_part, b_part = for_tiles(
            body, (jnp.full((SUBLANE, QB), jnp.inf, F32), jnp.full((SUBLANE, QB), -jnp.inf, F32)))
        return jnp.min(a_part, axis=0, keepdims=True), jnp.max(b_part, axis=0, keepdims=True)

    kf = jnp.minimum(qpos + 1, k_sel).astype(F32)
    top = jnp.max(hi_part, axis=0, keepdims=True)
    lo0 = jnp.min(lo_part, axis=0, keepdims=True)
    hi0 = top + (top - lo0) + jnp.maximum(jnp.abs(top), 1.0) * 1e-3
    clo0 = (qpos + 1).astype(F32)

    def split(st, mid):
        lo, hi, clo, chi = st
        c = count_ge(mid)
        ge = c >= kf
        return (jnp.where(ge, mid, lo), jnp.where(ge, hi, mid),
                jnp.where(ge, c, clo), jnp.where(ge, chi, c))

    def bisect(_, st):
        lo, hi = st[0], st[1]
        mid = lo + (hi - lo) * 0.5
        return split(st, jnp.where(mid <= lo, hi, mid))

    st = lax.fori_loop(0, BISECT_FIXED, bisect, (lo0, hi0, clo0, jnp.zeros((1, QB), F32)))

    def unresolved(a, b):
        return jnp.max(jnp.where(a < b, 1.0, 0.0))

    def refine(carry):
        st, a, b = carry[:4], carry[4], carry[5]
        mid = a + (b - a) * 0.5
        st = split(st, jnp.where(mid <= a, b, mid))
        a, b = bracket_ends(st[0], st[1])
        return st + (a, b, unresolved(a, b))

    a0, b0 = bracket_ends(st[0], st[1])
    lo, hi, clo, cgt, tau, _, _ = lax.while_loop(
        lambda c: c[6] > 0.0, refine, st + (a0, b0, unresolved(a0, b0)))

    need = kf - cgt

    @pl.when(jnp.max(jnp.where(clo > kf, 1.0, 0.0)) > 0.0)
    def _():
        tri = jnp.where(row >= lax.broadcasted_iota(jnp.int32, (KT, KT), 1), 1.0, 0.0).astype(BF16)

        def drop_tile(t, before):
            r0 = tile_start(t)
            s_t = s_ref[pl.ds(r0, KT), :]
            tied = jnp.where(s_t == tau, 1.0, 0.0)
            rank = before + jnp.dot(tri, tied.astype(BF16), preferred_element_type=F32)
            s_ref[pl.ds(r0, KT), :] = jnp.where(tied * rank > need, -inf, s_t)
            return before + jnp.sum(tied, axis=0, keepdims=True)

        lax.fori_loop(0, n_tiles, drop_tile, jnp.zeros((1, QB), F32))

    def logits_tile(t, mparts):
        r0 = tile_start(t)
        bias = jnp.where(s_ref[pl.ds(r0, KT), :] >= tau, 0.0, -inf)
        out = []
        for h in range(N_HEADS):
            hs = slice(h * HEAD_PAD, (h + 1) * HEAD_PAD)
            s = jnp.dot(kaug_ref[0, pl.ds(r0, KT), hs], qaugT_ref[0, hs, :],
                        preferred_element_type=F32) + bias
            lg_ref[h, pl.ds(r0, KT), :] = s
            out.append(jnp.maximum(mparts[h], jnp.max(s.reshape(groups, SUBLANE, QB), axis=0)))
        return tuple(out)

    mparts = for_tiles(logits_tile,
                       tuple(jnp.full((SUBLANE, QB), NEG_BIG, F32) for _ in range(N_HEADS)))
    m_rows = [jnp.max(mp, axis=0, keepdims=True) for mp in mparts]

    acc_ref[...] = jnp.zeros(acc_ref.shape, F32)

    ones_rows = jnp.ones((2 * SUBLANE, KT), BF16)

    def pv_tile(t, carry):
        r0 = tile_start(t)
        for h in range(N_HEADS):
            vs = slice(h * HEAD_DIM, (h + 1) * HEAD_DIM)
            ws = slice(h * VROWS, (h + 1) * VROWS)
            p = jnp.exp2((lg_ref[h, pl.ds(r0, KT), :] - m_rows[h]).astype(BF16))
            v_ones = jnp.concatenate([vT_ref[0, vs, pl.ds(r0, KT)], ones_rows], axis=0)
            acc_ref[ws, :] += jnp.dot(v_ones, p, preferred_element_type=F32)
        return carry

    for_tiles(pv_tile, 0)

    for h in range(N_HEADS):
        vs = slice(h * HEAD_DIM, (h + 1) * HEAD_DIM)
        denom = acc_ref[h * VROWS + HEAD_DIM:h * VROWS + HEAD_DIM + 1, :]
        out_ref[0, vs, :] = (acc_ref[h * VROWS:h * VROWS + HEAD_DIM, :] / denom).astype(BF16)


def _dsa(kw, qiT, wiT, kaug, qaugT, vT, k_sel):
    B, T, _ = kaug.shape
    assert T % QB == 0 and QB == KT
    att = N_HEADS * HEAD_DIM
    qblk = lambda w: pl.BlockSpec((1, w, QB), lambda b, i: (b, 0, i))
    return pl.pallas_call(
        functools.partial(_dsa_kernel, k_sel),
        grid=(B, T // QB),
        in_specs=[
            pl.BlockSpec((1, T, LANE), lambda b, i: (b, 0, 0)),
            qblk(IDX_HEADS * IDX_DIM),
            qblk(IDX_HEADS),
            pl.BlockSpec((1, T, N_HEADS * HEAD_PAD), lambda b, i: (b, 0, 0)),
            qblk(N_HEADS * HEAD_PAD),
            pl.BlockSpec((1, att, T), lambda b, i: (b, 0, 0)),
        ],
        out_specs=qblk(att),
        out_shape=jax.ShapeDtypeStruct((B, att, T), BF16),
        scratch_shapes=[
            pltpu.VMEM((T, QB), F32),
            pltpu.VMEM((N_HEADS, T, QB), F32),
            pltpu.VMEM((N_HEADS * VROWS, QB), F32),
        ],
        compiler_params=pltpu.CompilerParams(
            dimension_semantics=("arbitrary", "arbitrary"), vmem_limit_bytes=VMEM_LIMIT),
        name="dsa",
    )(kw, qiT, wiT, kaug, qaugT, vT)


def _out_ffn_kernel(tm, d_ff, x_ref, part_ref, g1_ref, attnT_ref, wb_ref, wo_ref, gffn_ref,
                    wup_ref, convf_ref, wdown_ref, out_ref, halo_ref, cbuf_ref):
    j = pl.program_id(1)
    yb = lax.dot_general(attnT_ref[0], wb_ref[...], (((0,), (0,)), ((), ())),
                         preferred_element_type=F32)
    merged = (part_ref[0].astype(F32) + g1_ref[0].astype(F32) * yb).astype(BF16)
    x1 = x_ref[0] + jnp.dot(merged, wo_ref[...], preferred_element_type=F32)
    xn2 = _rms_rows(x1, gffn_ref[...]).astype(BF16)

    @pl.when(j == 0)
    def _():
        halo_ref[...] = jnp.zeros_like(halo_ref)

    def up_proj(c):
        return tuple(jnp.dot(xn2, wup_ref[:, off:off + FF_CHUNK], preferred_element_type=F32)
                     for off in (c * FF_CHUNK, d_ff + c * FF_CHUNK))

    def conv_chunk(up, off, slot):
        cs = slice(off, off + FF_CHUNK)
        out = _causal_conv3(up, halo_ref[:, cs], convf_ref[:, cs], cbuf_ref, slot)
        halo_ref[:, cs] = up[tm - SUBLANE:tm, :]
        return out

    n_chunks = d_ff // FF_CHUNK
    blocks = FF_CHUNK // LANE
    acc = x1
    ups = up_proj(0)
    for c in range(n_chunks):
        nxt = up_proj(c + 1) if c + 1 < n_chunks else None
        slot = (c % FF_SLOTS) * 2 * blocks
        gate = conv_chunk(ups[0], c * FF_CHUNK, slot)
        val = conv_chunk(ups[1], d_ff + c * FF_CHUNK, slot + blocks)
        hmid = (gate * jax.nn.sigmoid(gate) * val).astype(BF16)
        acc = acc + jnp.dot(hmid, wdown_ref[c * FF_CHUNK:(c + 1) * FF_CHUNK, :],
                            preferred_element_type=F32)
        ups = nxt
    out_ref[0] = acc


def _out_ffn(x, part, g1, attnT, w_b_bf, w_o_bf, g_ffn, w_up_bf, conv_f_w, w_down_bf, tm):
    B, T, D = x.shape
    d_ff = w_down_bf.shape[0]
    att = N_HEADS * HEAD_DIM
    assert d_ff % FF_CHUNK == 0
    tok = lambda w: pl.BlockSpec((1, tm, w), lambda b, j: (b, j, 0))
    return pl.pallas_call(
        functools.partial(_out_ffn_kernel, tm, d_ff),
        grid=(B, T // tm),
        in_specs=[
            tok(D), tok(D), tok(D),
            pl.BlockSpec((1, att, tm), lambda b, j: (b, 0, j)),
            _const_spec((att, D)),
            _const_spec((D, D)),
            _const_spec((1, D)),
            _const_spec((D, 2 * d_ff)),
            _const_spec((3, 2 * d_ff)),
            _const_spec((d_ff, D)),
        ],
        out_specs=tok(D),
        out_shape=jax.ShapeDtypeStruct((B, T, D), F32),
        scratch_shapes=[pltpu.VMEM((SUBLANE, 2 * d_ff), F32),
                        pltpu.VMEM((FF_SLOTS * 2 * FF_CHUNK // LANE, tm + SUBLANE, LANE), F32)],
        compiler_params=pltpu.CompilerParams(
            dimension_semantics=("arbitrary", "arbitrary"), vmem_limit_bytes=VMEM_LIMIT),
        name="out_ffn",
    )(x, part, g1, attnT, w_b_bf, w_o_bf, g_ffn, w_up_bf, conv_f_w, w_down_bf)


def _layer(x, mem, g_mix, w_in, conv_a_w, w_a_out, q_norm_g, k_norm_g, w_b_out, g_mem, w_mem_kv,
           mq_norm_g, mk_norm_g, w_m_out, w_o, g_ffn, w_up, conv_f_w, w_down):
    B, T, D = x.shape
    tm_in = min(TM_IN, T)
    tm_out = min(TM_OUT, T)
    k_sel = min(TOPK_MAX, T // 4)
    row = lambda g: g.reshape(1, -1).astype(F32)

    kmT, vm = _mem_kv(mem, row(g_mem), w_mem_kv.astype(BF16), row(mk_norm_g))

    scale = HEAD_DIM ** -0.5
    gk_pad = jnp.pad(row(k_norm_g), ((0, 0), (0, HEAD_PAD - HEAD_DIM)))
    gqT = jnp.broadcast_to((q_norm_g.astype(F32) * (scale * LOG2E))[:, None], (HEAD_DIM, tm_in))
    part, g1, qaugT, kaug, vT, qiT, kw, wiT = _in_proj(
        x, row(g_mix), _relayout_w_in(w_in, D), conv_a_w, w_a_out.astype(BF16), gk_pad, gqT,
        kmT, vm, row(mq_norm_g), w_m_out.astype(BF16), tm_in)

    attnT = _dsa(kw, qiT, wiT, kaug, qaugT, vT, k_sel)

    return _out_ffn(x, part, g1, attnT, w_b_out.astype(BF16), w_o.astype(BF16), row(g_ffn),
                    w_up.astype(BF16), conv_f_w, w_down.astype(BF16), tm_out)


def kernel(x, mem, g_mix, w_in, conv_a_w, w_a_out, q_norm_g, k_norm_g, w_b_out, g_mem, w_mem_kv,
           mq_norm_g, mk_norm_g, w_m_out, w_o, g_ffn, w_up, conv_f_w, w_down):
    for l in range(g_mix.shape[0]):
        x = _layer(x, mem, g_mix[l], w_in[l], conv_a_w[l], w_a_out[l], q_norm_g[l], k_norm_g[l],
                   w_b_out[l], g_mem[l], w_mem_kv[l], mq_norm_g[l], mk_norm_g[l], w_m_out[l],
                   w_o[l], g_ffn[l], w_up[l], conv_f_w[l], w_down[l])
    return x
```

```python
import functools

import jax
import jax.numpy as jnp
import numpy as np
from jax import lax
from jax.experimental import pallas as pl
from jax.experimental.pallas import tpu as pltpu

F32 = jnp.float32
BF16 = jnp.bfloat16

EPS = 1e-6
N_HEADS = 8
HEAD_DIM = 64
IDX_HEADS = 8
IDX_DIM = 32
MEM_HEADS = 4
MEM_HEAD_DIM = 128
TOPK_MAX = 256
N_BRANCH = 3

LANE = 128
SUBLANE = 8
HEAD_PAD = LANE
POS_SPLIT = 8.0
LOG2E = 1.4426950408889634
SLOPE_TERMS = 3
VMEM_LIMIT = 60 * 1024 * 1024

TM_IN = 512
W_ROWS = 128
TM_OUT = 512
QB = 256
KT = 256
FF_CHUNK = 256
FF_SLOTS = 4
VROWS = HEAD_DIM + 16
BISECT_FIXED = 15
NEG_BIG = -1e30


def _const_spec(shape):
    nd = len(shape)
    return pl.BlockSpec(shape, lambda *_: (0,) * nd, pipeline_mode=pl.Buffered(1))


def _causal_conv3(u, prev8, w3, buf_ref, slot):
    rows = u.shape[0]
    outs = []
    for c in range(u.shape[1] // LANE):
        cs = slice(c * LANE, (c + 1) * LANE)
        buf_ref[slot + c, 0:SUBLANE, :] = prev8[:, cs]
        buf_ref[slot + c, SUBLANE:SUBLANE + rows, :] = u[:, cs]
        u1 = buf_ref[slot + c, pl.ds(SUBLANE - 1, rows), :]
        u2 = buf_ref[slot + c, pl.ds(SUBLANE - 2, rows), :]
        outs.append(w3[0:1, cs] * u2 + w3[1:2, cs] * u1 + w3[2:3, cs] * u[:, cs])
    return jnp.concatenate(outs, axis=1)


def _bf16_terms(c):
    rest, out = np.float32(c), []
    for _ in range(SLOPE_TERMS):
        term = np.float32(np.asarray(rest, dtype=jnp.bfloat16))
        out.append(float(term))
        rest = np.float32(rest - term)
    return out


def _rms_rows(x, g):
    return x * lax.rsqrt(jnp.mean(x * x, axis=-1, keepdims=True) + EPS) * g


def _mem_kv_kernel(mem_ref, gmem_ref, wkv_ref, gmk_ref, kmT_ref, vm_ref):
    memn = _rms_rows(mem_ref[0], gmem_ref[...]).astype(BF16)
    width = MEM_HEADS * MEM_HEAD_DIM
    km = jnp.dot(memn, wkv_ref[:, 0:width], preferred_element_type=F32)
    vm = jnp.dot(memn, wkv_ref[:, width:2 * width], preferred_element_type=F32)
    heads = []
    for h in range(MEM_HEADS):
        kh = km[:, h * MEM_HEAD_DIM:(h + 1) * MEM_HEAD_DIM]
        heads.append(_rms_rows(kh, gmk_ref[...]))
    kmT_ref[0] = jnp.concatenate(heads, axis=1).T.astype(BF16)
    vm_ref[0] = vm.astype(BF16)


def _mem_kv(mem, g_mem, w_kv_bf, mk_g):
    B, M, D = mem.shape
    width = MEM_HEADS * MEM_HEAD_DIM
    return pl.pallas_call(
        _mem_kv_kernel,
        grid=(B,),
        in_specs=[
            pl.BlockSpec((1, M, D), lambda b: (b, 0, 0)),
            _const_spec((1, D)),
            _const_spec((D, 2 * width)),
            _const_spec((1, MEM_HEAD_DIM)),
        ],
        out_specs=[
            pl.BlockSpec((1, width, M), lambda b: (b, 0, 0)),
            pl.BlockSpec((1, M, width), lambda b: (b, 0, 0)),
        ],
        out_shape=[
            jax.ShapeDtypeStruct((B, width, M), BF16),
            jax.ShapeDtypeStruct((B, M, width), BF16),
        ],
        compiler_params=pltpu.CompilerParams(
            dimension_semantics=("arbitrary",), vmem_limit_bytes=VMEM_LIMIT),
        name="mem_kv",
    )(mem, g_mem, w_kv_bf, mk_g)


class _InCols:
    def __init__(self, d_model):
        sc = d_model // 2
        self.sc = sc
        self.bch = 0
        self.q = self.bch + 3 * sc
        self.k = self.q + N_HEADS * HEAD_DIM
        self.v = self.k + N_HEADS * HEAD_DIM
        self.qi = self.v + N_HEADS * HEAD_DIM
        self.main = self.qi + IDX_HEADS * IDX_DIM


def _split_w_in_kernel(cols, w_ref, main_ref, kw_ref, qm_ref, gates_ref):
    w = w_ref[0]
    kw_end = cols.main + IDX_DIM + IDX_HEADS
    qm_end = kw_end + MEM_HEADS * MEM_HEAD_DIM
    main_ref[...] = w[:, :cols.main].astype(BF16)
    kw_pad = jnp.zeros((w.shape[0], LANE - IDX_DIM - IDX_HEADS), F32)
    kw_ref[...] = jnp.concatenate([w[:, cols.main:kw_end], kw_pad], axis=1).astype(BF16)
    qm_ref[...] = w[:, kw_end:qm_end].astype(BF16)
    gates_ref[...] = w[:, qm_end:].astype(BF16)


def _split_w_in(w_in3):
    _, D, width = w_in3.shape
    cols = _InCols(D)
    widths = (cols.main, LANE, MEM_HEADS * MEM_HEAD_DIM, N_BRANCH * D)
    assert cols.main + IDX_DIM + IDX_HEADS + widths[2] + widths[3] == width
    return pl.pallas_call(
        functools.partial(_split_w_in_kernel, cols),
        grid=(D // W_ROWS,),
        in_specs=[pl.BlockSpec((1, W_ROWS, width), lambda r: (0, r, 0))],
        out_specs=[pl.BlockSpec((W_ROWS, w), lambda r: (r, 0)) for w in widths],
        out_shape=[jax.ShapeDtypeStruct((D, w), BF16) for w in widths],
        compiler_params=pltpu.CompilerParams(
            dimension_semantics=("arbitrary",), vmem_limit_bytes=VMEM_LIMIT),
        name="split_w_in",
    )(w_in3)


def _in_proj_kernel(cols, tm, d_model,
                    x_ref, gmix_ref, w_ref, wkw_ref, wqm_ref, wg_ref, convw_ref, waout_ref, gk_ref, gqT_ref,
                    kmT_ref, vm_ref, gmq_ref, wmout_ref,
                    part_ref, g1_ref, qaugT_ref, kaug_ref, vT_ref, qiT_ref, kw_ref, wiT_ref,
                    halo_ref, cbuf_ref):
    j = pl.program_id(1)
    sc = cols.sc
    xn = _rms_rows(x_ref[0], gmix_ref[...]).astype(BF16)

    def proj(off, width, ref=w_ref):
        return jnp.dot(xn, ref[:, off:off + width], preferred_element_type=F32)

    @pl.when(j == 0)
    def _():
        halo_ref[...] = jnp.zeros_like(halo_ref)

    ch = proj(cols.bch + sc, sc) * proj(cols.bch + 2 * sc, sc)
    conv = _causal_conv3(ch, halo_ref[...], convw_ref[...], cbuf_ref, 0)
    halo_ref[...] = ch[tm - SUBLANE:tm, :]
    ua = (proj(cols.bch, sc) * conv).astype(BF16)
    ya = jnp.dot(ua, waout_ref[...], preferred_element_type=F32)

    qT = proj(cols.q, N_HEADS * HEAD_DIM).T
    row8 = lax.broadcasted_iota(jnp.int32, (SUBLANE, tm), 0)
    zpad = jnp.zeros((HEAD_PAD - HEAD_DIM - SUBLANE, tm), F32)
    for h in range(N_HEADS):
        blk = qT[h * HEAD_DIM:(h + 1) * HEAD_DIM, :]
        r = lax.rsqrt(jnp.sum(blk * blk, axis=0, keepdims=True) * (1.0 / HEAD_DIM) + EPS)
        aug = jnp.zeros((SUBLANE, tm), F32)
        for i, term in enumerate(_bf16_terms(2.0 ** (-8.0 * (h + 1) / N_HEADS) * LOG2E)):
            aug = jnp.where((row8 == i) | (row8 == i + SLOPE_TERMS), term, aug)
        full = jnp.concatenate([blk * r * gqT_ref[...], aug, zpad], axis=0)
        qaugT_ref[0, h * HEAD_PAD:(h + 1) * HEAD_PAD, :] = full.astype(BF16)

    kraw = proj(cols.k, N_HEADS * HEAD_DIM)
    posf = (j * tm + lax.broadcasted_iota(jnp.int32, (tm, HEAD_PAD), 0)).astype(F32)
    lane = lax.broadcasted_iota(jnp.int32, (tm, HEAD_PAD), 1)
    pos_hi = jnp.floor(posf * (1.0 / POS_SPLIT)) * POS_SPLIT
    in_hi = (lane >= HEAD_DIM) & (lane < HEAD_DIM + SLOPE_TERMS)
    in_lo = (lane >= HEAD_DIM + SLOPE_TERMS) & (lane < HEAD_DIM + 2 * SLOPE_TERMS)
    posmat = jnp.where(in_hi, pos_hi, jnp.where(in_lo, posf - pos_hi, 0.0))
    for h in range(N_HEADS):
        pair = kraw[:, (h // 2) * LANE:(h // 2 + 1) * LANE]
        if h % 2:
            pair = pltpu.roll(pair, HEAD_DIM, axis=1)
        kg = jnp.where(lane < HEAD_DIM, pair, 0.0)
        r = lax.rsqrt(jnp.sum(kg * kg, axis=-1, keepdims=True) * (1.0 / HEAD_DIM) + EPS)
        kaug_ref[0, :, h * HEAD_PAD:(h + 1) * HEAD_PAD] = (kg * r * gk_ref[...] + posmat).astype(BF16)

    vT_ref[0] = proj(cols.v, N_HEADS * HEAD_DIM).T.astype(BF16)
    qiT_ref[0] = proj(cols.qi, IDX_HEADS * IDX_DIM).T.astype(BF16)
    kw = proj(0, LANE, wkw_ref)
    kw_ref[0] = kw.astype(BF16)
    wiT_ref[0] = kw.T[IDX_DIM:IDX_DIM + IDX_HEADS, :]

    qm = proj(0, MEM_HEADS * MEM_HEAD_DIM, wqm_ref)
    mem_scale = MEM_HEAD_DIM ** -0.5
    heads = []
    for h in range(MEM_HEADS):
        sl = slice(h * MEM_HEAD_DIM, (h + 1) * MEM_HEAD_DIM)
        qh = _rms_rows(qm[:, sl], gmq_ref[...]).astype(BF16)
        lg = jnp.dot(qh, kmT_ref[0, sl, :], preferred_element_type=F32) * mem_scale
        p = jnp.exp(lg - jnp.max(lg, axis=-1, keepdims=True))
        denom = jnp.sum(p, axis=-1, keepdims=True)
        oh = jnp.dot(p.astype(BF16), vm_ref[0, :, sl], preferred_element_type=F32)
        heads.append(oh / denom)
    om = jnp.concatenate(heads, axis=1).astype(BF16)
    ym = jnp.dot(om, wmout_ref[...], preferred_element_type=F32)

    g0 = jax.nn.sigmoid(proj(0, d_model, wg_ref))
    g2 = jax.nn.sigmoid(proj(2 * d_model, d_model, wg_ref))
    part_ref[0] = (g0 * ya + g2 * ym).astype(BF16)
    g1_ref[0] = jax.nn.sigmoid(proj(d_model, d_model, wg_ref)).astype(BF16)


def _in_proj(x, g_mix, w_pieces, conv_a_w, w_a_out_bf, gk_pad, gqT, kmT, vm, mq_g, w_m_out_bf, tm):
    B, T, D = x.shape
    cols = _InCols(D)
    M = vm.shape[1]
    memw = MEM_HEADS * MEM_HEAD_DIM
    tok = lambda w: pl.BlockSpec((1, tm, w), lambda b, j: (b, j, 0))
    tokT = lambda w: pl.BlockSpec((1, w, tm), lambda b, j: (b, 0, j))
    return pl.pallas_call(
        functools.partial(_in_proj_kernel, cols, tm, D),
        grid=(B, T // tm),
        in_specs=[
            tok(D),
            _const_spec((1, D)),
            _const_spec((D, cols.main)),
            _const_spec((D, LANE)),
            _const_spec((D, memw)),
            _const_spec((D, N_BRANCH * D)),
            _const_spec((3, cols.sc)),
            _const_spec((cols.sc, D)),
            _const_spec((1, HEAD_PAD)),
            _const_spec((HEAD_DIM, tm)),
            pl.BlockSpec((1, memw, M), lambda b, j: (b, 0, 0)),
            pl.BlockSpec((1, M, memw), lambda b, j: (b, 0, 0)),
            _const_spec((1, MEM_HEAD_DIM)),
            _const_spec((memw, D)),
        ],
        out_specs=[
            tok(D), tok(D),
            tokT(N_HEADS * HEAD_PAD), tok(N_HEADS * HEAD_PAD),
            tokT(N_HEADS * HEAD_DIM), tokT(IDX_HEADS * IDX_DIM),
            tok(LANE), tokT(IDX_HEADS),
        ],
        out_shape=[
            jax.ShapeDtypeStruct((B, T, D), BF16),
            jax.ShapeDtypeStruct((B, T, D), BF16),
            jax.ShapeDtypeStruct((B, N_HEADS * HEAD_PAD, T), BF16),
            jax.ShapeDtypeStruct((B, T, N_HEADS * HEAD_PAD), BF16),
            jax.ShapeDtypeStruct((B, N_HEADS * HEAD_DIM, T), BF16),
            jax.ShapeDtypeStruct((B, IDX_HEADS * IDX_DIM, T), BF16),
            jax.ShapeDtypeStruct((B, T, LANE), BF16),
            jax.ShapeDtypeStruct((B, IDX_HEADS, T), F32),
        ],
        scratch_shapes=[pltpu.VMEM((SUBLANE, cols.sc), F32),
                        pltpu.VMEM((cols.sc // LANE, tm + SUBLANE, LANE), F32)],
        compiler_params=pltpu.CompilerParams(
            dimension_semantics=("arbitrary", "arbitrary"), vmem_limit_bytes=VMEM_LIMIT),
        name="in_proj",
    )(x, g_mix, *w_pieces, conv_a_w, w_a_out_bf, gk_pad, gqT, kmT, vm, mq_g, w_m_out_bf)


def _dsa_kernel(k_sel, kw_ref, qiT_ref, wiT_ref, kaug_ref, qaugT_ref, vT_ref, out_ref,
                s_ref, lg_ref, acc_ref):
    i = pl.program_id(1)
    n_tiles = i + 1
    qpos = i * QB + lax.broadcasted_iota(jnp.int32, (1, QB), 1)
    row = lax.broadcasted_iota(jnp.int32, (KT, QB), 0)
    groups = KT // SUBLANE
    inf = jnp.float32(jnp.inf)

    def tile_start(t):
        return pl.multiple_of(t * KT, KT)

    def for_tiles(body, init, group=2):
        def run(t0, count, carry):
            for j in range(count):
                carry = body(t0 + j, carry)
            return carry
        carry = lax.fori_loop(0, n_tiles // group, lambda p, c: run(group * p, group, c), init)
        done = (n_tiles // group) * group
        part = group // 2
        while part >= 1:
            take = (n_tiles & part) != 0
            carry = lax.cond(take, lambda c, d=done, n=part: run(d, n, c), lambda c: c, carry)
            done = done + jnp.where(take, part, 0)
            part //= 2
        return carry

    zrows = jnp.zeros((LANE - IDX_DIM, QB), BF16)

    def score_tile(t, carry):
        lo_part, hi_part = carry
        r0 = tile_start(t)
        kw_t = kw_ref[0, pl.ds(r0, KT), :]
        acc = jnp.zeros((KT, QB), F32)
        for h in range(IDX_HEADS):
            rhs = jnp.concatenate([qiT_ref[0, h * IDX_DIM:(h + 1) * IDX_DIM, :], zrows], axis=0)
            lg = jnp.dot(kw_t, rhs, preferred_element_type=F32)
            acc = acc + wiT_ref[0, h:h + 1, :] * jnp.maximum(lg, 0.0)
        valid = r0 + row <= qpos
        s_ref[pl.ds(r0, KT), :] = jnp.where(valid, acc, -inf)
        lo_t = jnp.min(jnp.where(valid, acc, inf).reshape(groups, SUBLANE, QB), axis=0)
        hi_t = jnp.max(jnp.where(valid, acc, -inf).reshape(groups, SUBLANE, QB), axis=0)
        return jnp.minimum(lo_part, lo_t), jnp.maximum(hi_part, hi_t)

    lo_part, hi_part = for_tiles(
        score_tile,
        (jnp.full((SUBLANE, QB), jnp.inf, F32), jnp.full((SUBLANE, QB), -jnp.inf, F32)), group=4)

    def tree(x3, op):
        while x3.shape[0] > 1:
            half = x3.shape[0] // 2
            x3 = op(x3[:half], x3[half:])
        return x3[0]

    def load3(t):
        return s_ref[pl.ds(tile_start(t), KT), :].reshape(groups, SUBLANE, QB)

    def bcast8(v):
        return jnp.broadcast_to(v, (SUBLANE, QB))[None]

    def count_ge(v):
        v8 = bcast8(v)

        def body(t, part):
            return part + tree(jnp.where(load3(t) >= v8, 1.0, 0.0), jnp.add)

        part = for_tiles(body, jnp.zeros((SUBLANE, QB), F32))
        return jnp.sum(part, axis=0, keepdims=True)

    def bracket_ends(lo, hi):
        lo8, hi8 = bcast8(lo), bcast8(hi)

        def body(t, carry):
            s3 = load3(t)
            a_t = tree(jnp.where(s3 >= lo8, s3, inf), jnp.minimum)
            b_t = tree(jnp.where(s3 < hi8, s3, -inf), jnp.maximum)
            return jnp.minimum(carry[0], a_t), jnp.maximum(carry[1], b_t)

        a_part, b_part = for_tiles(
            body, (jnp.full((SUBLANE, QB), jnp.inf, F32), jnp.full((SUBLANE, QB), -jnp.inf, F32)))
        return jnp.min(a_part, axis=0, keepdims=True), jnp.max(b_part, axis=0, keepdims=True)

    kf = jnp.minimum(qpos + 1, k_sel).astype(F32)
    top = jnp.max(hi_part, axis=0, keepdims=True)
    lo0 = jnp.min(lo_part, axis=0, keepdims=True)
    hi0 = top + jnp.maximum(jnp.abs(top), 1.0) * 1e-3
    clo0 = (qpos + 1).astype(F32)

    def split(st, mid):
        lo, hi, clo, chi = st
        c = count_ge(mid)
        ge = c >= kf
        return (jnp.where(ge, mid, lo), jnp.where(ge, hi, mid),
                jnp.where(ge, c, clo), jnp.where(ge, chi, c))

    def bisect(_, st):
        lo, hi = st[0], st[1]
        mid = lo + (hi - lo) * 0.5
        return split(st, jnp.where(mid <= lo, hi, mid))

    st = lax.fori_loop(0, BISECT_FIXED, bisect, (lo0, hi0, clo0, jnp.zeros((1, QB), F32)))

    def unresolved(a, b):
        return jnp.max(jnp.where(a < b, 1.0, 0.0))

    def refine(carry):
        st, a, b = carry[:4], carry[4], carry[5]
        mid = a + (b - a) * 0.5
        st = split(st, jnp.where(mid <= a, b, mid))
        a, b = bracket_ends(st[0], st[1])
        return st + (a, b, unresolved(a, b))

    a0, b0 = bracket_ends(st[0], st[1])
    lo, hi, clo, cgt, tau, _, _ = lax.while_loop(
        lambda c: c[6] > 0.0, refine, st + (a0, b0, unresolved(a0, b0)))

    need = kf - cgt

    @pl.when(jnp.max(jnp.where(clo > kf, 1.0, 0.0)) > 0.0)
    def _():
        tri = jnp.where(row >= lax.broadcasted_iota(jnp.int32, (KT, KT), 1), 1.0, 0.0).astype(BF16)

        def drop_tile(t, before):
            r0 = tile_start(t)
            s_t = s_ref[pl.ds(r0, KT), :]
            tied = jnp.where(s_t == tau, 1.0, 0.0)
            rank = before + jnp.dot(tri, tied.astype(BF16), preferred_element_type=F32)
            s_ref[pl.ds(r0, KT), :] = jnp.where(tied * rank > need, -inf, s_t)
            return before + jnp.sum(tied, axis=0, keepdims=True)

        lax.fori_loop(0, n_tiles, drop_tile, jnp.zeros((1, QB), F32))

    def logits_tile(t, mparts):
        r0 = tile_start(t)
        bias = jnp.where(s_ref[pl.ds(r0, KT), :] >= tau, 0.0, -inf)
        out = []
        for h in range(N_HEADS):
            hs = slice(h * HEAD_PAD, (h + 1) * HEAD_PAD)
            s = jnp.dot(kaug_ref[0, pl.ds(r0, KT), hs], qaugT_ref[0, hs, :],
                        preferred_element_type=F32) + bias
            lg_ref[h, pl.ds(r0, KT), :] = s
            out.append(jnp.maximum(mparts[h], jnp.max(s.reshape(groups, SUBLANE, QB), axis=0)))
        return tuple(out)

    mparts = for_tiles(logits_tile,
                       tuple(jnp.full((SUBLANE, QB), NEG_BIG, F32) for _ in range(N_HEADS)), group=4)
    m_rows = [jnp.max(mp, axis=0, keepdims=True) for mp in mparts]

    acc_ref[...] = jnp.zeros(acc_ref.shape, F32)

    ones_rows = jnp.ones((2 * SUBLANE, KT), BF16)

    def pv_tile(t, carry):
        r0 = tile_start(t)
        for h in range(N_HEADS):
            vs = slice(h * HEAD_DIM, (h + 1) * HEAD_DIM)
            ws = slice(h * VROWS, (h + 1) * VROWS)
            p = jnp.exp2((lg_ref[h, pl.ds(r0, KT), :] - m_rows[h]).astype(BF16))
            v_ones = jnp.concatenate([vT_ref[0, vs, pl.ds(r0, KT)], ones_rows], axis=0)
            acc_ref[ws, :] += jnp.dot(v_ones, p, preferred_element_type=F32)
        return carry

    for_tiles(pv_tile, 0, group=4)

    for h in range(N_HEADS):
        vs = slice(h * HEAD_DIM, (h + 1) * HEAD_DIM)
        denom = acc_ref[h * VROWS + HEAD_DIM:h * VROWS + HEAD_DIM + 1, :]
        out_ref[0, vs, :] = (acc_ref[h * VROWS:h * VROWS + HEAD_DIM, :] / denom).astype(BF16)


def _dsa(kw, qiT, wiT, kaug, qaugT, vT, k_sel):
    B, T, _ = kaug.shape
    assert T % QB == 0 and QB == KT
    att = N_HEADS * HEAD_DIM
    qblk = lambda w: pl.BlockSpec((1, w, QB), lambda b, i: (b, 0, i))
    return pl.pallas_call(
        functools.partial(_dsa_kernel, k_sel),
        grid=(B, T // QB),
        in_specs=[
            pl.BlockSpec((1, T, LANE), lambda b, i: (b, 0, 0)),
            qblk(IDX_HEADS * IDX_DIM),
            qblk(IDX_HEADS),
            pl.BlockSpec((1, T, N_HEADS * HEAD_PAD), lambda b, i: (b, 0, 0)),
            qblk(N_HEADS * HEAD_PAD),
            pl.BlockSpec((1, att, T), lambda b, i: (b, 0, 0)),
        ],
        out_specs=qblk(att),
        out_shape=jax.ShapeDtypeStruct((B, att, T), BF16),
        scratch_shapes=[
            pltpu.VMEM((T, QB), F32),
            pltpu.VMEM((N_HEADS, T, QB), F32),
            pltpu.VMEM((N_HEADS * VROWS, QB), F32),
        ],
        compiler_params=pltpu.CompilerParams(
            dimension_semantics=("arbitrary", "arbitrary"), vmem_limit_bytes=VMEM_LIMIT),
        name="dsa",
    )(kw, qiT, wiT, kaug, qaugT, vT)


def _out_ffn_kernel(tm, d_ff, x_ref, part_ref, g1_ref, attnT_ref, wb_ref, wo_ref, gffn_ref,
                    wup_ref, convf_ref, wdown_ref, out_ref, halo_ref, cbuf_ref):
    j = pl.program_id(1)
    yb = lax.dot_general(attnT_ref[0], wb_ref[...], (((0,), (0,)), ((), ())),
                         preferred_element_type=F32)
    merged = (part_ref[0].astype(F32) + g1_ref[0].astype(F32) * yb).astype(BF16)
    x1 = x_ref[0] + jnp.dot(merged, wo_ref[...], preferred_element_type=F32)
    xn2 = _rms_rows(x1, gffn_ref[...]).astype(BF16)

    @pl.when(j == 0)
    def _():
        halo_ref[...] = jnp.zeros_like(halo_ref)

    def up_proj(c):
        return tuple(jnp.dot(xn2, wup_ref[:, off:off + FF_CHUNK], preferred_element_type=F32)
                     for off in (c * FF_CHUNK, d_ff + c * FF_CHUNK))

    def conv_chunk(up, off, slot):
        cs = slice(off, off + FF_CHUNK)
        out = _causal_conv3(up, halo_ref[:, cs], convf_ref[:, cs], cbuf_ref, slot)
        halo_ref[:, cs] = up[tm - SUBLANE:tm, :]
        return out

    n_chunks = d_ff // FF_CHUNK
    blocks = FF_CHUNK // LANE
    acc = x1
    ups = up_proj(0)
    for c in range(n_chunks):
        nxt = up_proj(c + 1) if c + 1 < n_chunks else None
        slot = (c % FF_SLOTS) * 2 * blocks
        gate = conv_chunk(ups[0], c * FF_CHUNK, slot)
        val = conv_chunk(ups[1], d_ff + c * FF_CHUNK, slot + blocks)
        hmid = (gate * jax.nn.sigmoid(gate) * val).astype(BF16)
        acc = acc + jnp.dot(hmid, wdown_ref[c * FF_CHUNK:(c + 1) * FF_CHUNK, :],
                            preferred_element_type=F32)
        ups = nxt
    out_ref[0] = acc


def _out_ffn(x, part, g1, attnT, w_b_bf, w_o_bf, g_ffn, w_up_bf, conv_f_w, w_down_bf, tm):
    B, T, D = x.shape
    d_ff = w_down_bf.shape[0]
    att = N_HEADS * HEAD_DIM
    assert d_ff % FF_CHUNK == 0
    tok = lambda w: pl.BlockSpec((1, tm, w), lambda b, j: (b, j, 0))
    return pl.pallas_call(
        functools.partial(_out_ffn_kernel, tm, d_ff),
        grid=(B, T // tm),
        in_specs=[
            tok(D), tok(D), tok(D),
            pl.BlockSpec((1, att, tm), lambda b, j: (b, 0, j)),
            _const_spec((att, D)),
            _const_spec((D, D)),
            _const_spec((1, D)),
            _const_spec((D, 2 * d_ff)),
            _const_spec((3, 2 * d_ff)),
            _const_spec((d_ff, D)),
        ],
        out_specs=tok(D),
        out_shape=jax.ShapeDtypeStruct((B, T, D), F32),
        scratch_shapes=[pltpu.VMEM((SUBLANE, 2 * d_ff), F32),
                        pltpu.VMEM((FF_SLOTS * 2 * FF_CHUNK // LANE, tm + SUBLANE, LANE), F32)],
        compiler_params=pltpu.CompilerParams(
            dimension_semantics=("arbitrary", "arbitrary"), vmem_limit_bytes=VMEM_LIMIT),
        name="out_ffn",
    )(x, part, g1, attnT, w_b_bf, w_o_bf, g_ffn, w_up_bf, conv_f_w, w_down_bf)


def _layer(x, mem, g_mix, w_in, conv_a_w, w_a_out, q_norm_g, k_norm_g, w_b_out, g_mem, w_mem_kv,
           mq_norm_g, mk_norm_g, w_m_out, w_o, g_ffn, w_up, conv_f_w, w_down):
    B, T, D = x.shape
    tm_in = min(TM_IN, T)
    tm_out = min(TM_OUT, T)
    k_sel = min(TOPK_MAX, T // 4)
    row = lambda g: g.reshape(1, -1).astype(F32)

    kmT, vm = _mem_kv(mem, row(g_mem), w_mem_kv.astype(BF16), row(mk_norm_g))

    scale = HEAD_DIM ** -0.5
    gk_pad = jnp.pad(row(k_norm_g), ((0, 0), (0, HEAD_PAD - HEAD_DIM)))
    gqT = jnp.broadcast_to((q_norm_g.astype(F32) * (scale * LOG2E))[:, None], (HEAD_DIM, tm_in))
    part, g1, qaugT, kaug, vT, qiT, kw, wiT = _in_proj(
        x, row(g_mix), _split_w_in(w_in[None]), conv_a_w, w_a_out.astype(BF16), gk_pad, gqT,
        kmT, vm, row(mq_norm_g), w_m_out.astype(BF16), tm_in)

    attnT = _dsa(kw, qiT, wiT, kaug, qaugT, vT, k_sel)

    return _out_ffn(x, part, g1, attnT, w_b_out.astype(BF16), w_o.astype(BF16), row(g_ffn),
                    w_up.astype(BF16), conv_f_w, w_down.astype(BF16), tm_out)


def kernel(x, mem, g_mix, w_in, conv_a_w, w_a_out, q_norm_g, k_norm_g, w_b_out, g_mem, w_mem_kv,
           mq_norm_g, mk_norm_g, w_m_out, w_o, g_ffn, w_up, conv_f_w, w_down):
    for l in range(g_mix.shape[0]):
        x = _layer(x, mem, g_mix[l], w_in[l], conv_a_w[l], w_a_out[l], q_norm_g[l], k_norm_g[l],
                   w_b_out[l], g_mem[l], w_mem_kv[l], mq_norm_g[l], mk_norm_g[l], w_m_out[l],
                   w_o[l], g_ffn[l], w_up[l], conv_f_w[l], w_down[l])
    return x
```

```python
import functools

import jax
import jax.numpy as jnp
import numpy as np
from jax import lax
from jax.experimental import pallas as pl
from jax.experimental.pallas import tpu as pltpu

F32 = jnp.float32
BF16 = jnp.bfloat16

EPS = 1e-6
N_HEADS = 8
HEAD_DIM = 64
IDX_HEADS = 8
IDX_DIM = 32
MEM_HEADS = 4
MEM_HEAD_DIM = 128
TOPK_MAX = 256
N_BRANCH = 3

LANE = 128
SUBLANE = 8
HEAD_PAD = LANE
POS_SPLIT = 8.0
LOG2E = 1.4426950408889634
SLOPE_TERMS = 3
VMEM_LIMIT = 60 * 1024 * 1024

TM_IN = 512
W_ROWS = 128
TM_OUT = 512
QB = 256
KT = 256
FF_CHUNK = 256
FF_SLOTS = 4
VROWS = HEAD_DIM + 16
BISECT_FIXED = 15
NEG_BIG = -1e30


def _const_spec(shape):
    nd = len(shape)
    return pl.BlockSpec(shape, lambda *_: (0,) * nd, pipeline_mode=pl.Buffered(1))


def _causal_conv3(u, prev8, w3, buf_ref, slot):
    rows = u.shape[0]
    outs = []
    for c in range(u.shape[1] // LANE):
        cs = slice(c * LANE, (c + 1) * LANE)
        buf_ref[slot + c, 0:SUBLANE, :] = prev8[:, cs]
        buf_ref[slot + c, SUBLANE:SUBLANE + rows, :] = u[:, cs]
        u1 = buf_ref[slot + c, pl.ds(SUBLANE - 1, rows), :]
        u2 = buf_ref[slot + c, pl.ds(SUBLANE - 2, rows), :]
        outs.append(w3[0:1, cs] * u2 + w3[1:2, cs] * u1 + w3[2:3, cs] * u[:, cs])
    return jnp.concatenate(outs, axis=1)


def _bf16_terms(c):
    rest, out = np.float32(c), []
    for _ in range(SLOPE_TERMS):
        term = np.float32(np.asarray(rest, dtype=jnp.bfloat16))
        out.append(float(term))
        rest = np.float32(rest - term)
    return out


def _rms_rows(x, g):
    return x * lax.rsqrt(jnp.mean(x * x, axis=-1, keepdims=True) + EPS) * g


def _mem_kv_kernel(mem_ref, gmem_ref, wkv_ref, gmk_ref, kmT_ref, vm_ref):
    memn = _rms_rows(mem_ref[0], gmem_ref[...]).astype(BF16)
    width = MEM_HEADS * MEM_HEAD_DIM
    km = jnp.dot(memn, wkv_ref[:, 0:width], preferred_element_type=F32)
    vm = jnp.dot(memn, wkv_ref[:, width:2 * width], preferred_element_type=F32)
    heads = []
    for h in range(MEM_HEADS):
        kh = km[:, h * MEM_HEAD_DIM:(h + 1) * MEM_HEAD_DIM]
        heads.append(_rms_rows(kh, gmk_ref[...]))
    kmT_ref[0] = jnp.concatenate(heads, axis=1).T.astype(BF16)
    vm_ref[0] = vm.astype(BF16)


def _mem_kv(mem, g_mem, w_kv_bf, mk_g):
    B, M, D = mem.shape
    width = MEM_HEADS * MEM_HEAD_DIM
    return pl.pallas_call(
        _mem_kv_kernel,
        grid=(B,),
        in_specs=[
            pl.BlockSpec((1, M, D), lambda b: (b, 0, 0)),
            _const_spec((1, D)),
            _const_spec((D, 2 * width)),
            _const_spec((1, MEM_HEAD_DIM)),
        ],
        out_specs=[
            pl.BlockSpec((1, width, M), lambda b: (b, 0, 0)),
            pl.BlockSpec((1, M, width), lambda b: (b, 0, 0)),
        ],
        out_shape=[
            jax.ShapeDtypeStruct((B, width, M), BF16),
            jax.ShapeDtypeStruct((B, M, width), BF16),
        ],
        compiler_params=pltpu.CompilerParams(
            dimension_semantics=("arbitrary",), vmem_limit_bytes=VMEM_LIMIT),
        name="mem_kv",
    )(mem, g_mem, w_kv_bf, mk_g)


class _InCols:
    def __init__(self, d_model):
        sc = d_model // 2
        self.sc = sc
        self.bch = 0
        self.q = self.bch + 3 * sc
        self.k = self.q + N_HEADS * HEAD_DIM
        self.v = self.k + N_HEADS * HEAD_DIM
        self.qi = self.v + N_HEADS * HEAD_DIM
        self.main = self.qi + IDX_HEADS * IDX_DIM


def _split_w_in_kernel(cols, w_ref, main_ref, kw_ref, qm_ref, gates_ref):
    w = w_ref[0]
    kw_end = cols.main + IDX_DIM + IDX_HEADS
    qm_end = kw_end + MEM_HEADS * MEM_HEAD_DIM
    main_ref[...] = w[:, :cols.main].astype(BF16)
    kw_pad = jnp.zeros((w.shape[0], LANE - IDX_DIM - IDX_HEADS), F32)
    kw_ref[...] = jnp.concatenate([w[:, cols.main:kw_end], kw_pad], axis=1).astype(BF16)
    qm_ref[...] = w[:, kw_end:qm_end].astype(BF16)
    gates_ref[...] = w[:, qm_end:].astype(BF16)


def _split_w_in(w_in_stack, layer):
    _, D, width = w_in_stack.shape
    cols = _InCols(D)
    widths = (cols.main, LANE, MEM_HEADS * MEM_HEAD_DIM, N_BRANCH * D)
    assert cols.main + IDX_DIM + IDX_HEADS + widths[2] + widths[3] == width
    return pl.pallas_call(
        functools.partial(_split_w_in_kernel, cols),
        grid=(D // W_ROWS,),
        in_specs=[pl.BlockSpec((1, W_ROWS, width), lambda r: (layer, r, 0))],
        out_specs=[pl.BlockSpec((W_ROWS, w), lambda r: (r, 0)) for w in widths],
        out_shape=[jax.ShapeDtypeStruct((D, w), BF16) for w in widths],
        compiler_params=pltpu.CompilerParams(
            dimension_semantics=("arbitrary",), vmem_limit_bytes=VMEM_LIMIT),
        name="split_w_in",
    )(w_in_stack)


def _in_proj_kernel(cols, tm, d_model,
                    x_ref, gmix_ref, w_ref, wkw_ref, wqm_ref, wg_ref, convw_ref, waout_ref, gk_ref, gqT_ref,
                    kmT_ref, vm_ref, gmq_ref, wmout_ref,
                    part_ref, g1_ref, qaugT_ref, kaug_ref, vT_ref, qiT_ref, kw_ref, wiT_ref,
                    halo_ref, cbuf_ref):
    j = pl.program_id(1)
    sc = cols.sc
    xn = _rms_rows(x_ref[0], gmix_ref[...]).astype(BF16)

    def proj(off, width, ref=w_ref):
        return jnp.dot(xn, ref[:, off:off + width], preferred_element_type=F32)

    @pl.when(j == 0)
    def _():
        halo_ref[...] = jnp.zeros_like(halo_ref)

    ch = proj(cols.bch + sc, sc) * proj(cols.bch + 2 * sc, sc)
    conv = _causal_conv3(ch, halo_ref[...], convw_ref[...], cbuf_ref, 0)
    halo_ref[...] = ch[tm - SUBLANE:tm, :]
    ua = (proj(cols.bch, sc) * conv).astype(BF16)
    ya = jnp.dot(ua, waout_ref[...], preferred_element_type=F32)

    qT = proj(cols.q, N_HEADS * HEAD_DIM).T
    row8 = lax.broadcasted_iota(jnp.int32, (SUBLANE, tm), 0)
    zpad = jnp.zeros((HEAD_PAD - HEAD_DIM - SUBLANE, tm), F32)
    for h in range(N_HEADS):
        blk = qT[h * HEAD_DIM:(h + 1) * HEAD_DIM, :]
        r = lax.rsqrt(jnp.sum(blk * blk, axis=0, keepdims=True) * (1.0 / HEAD_DIM) + EPS)
        aug = jnp.zeros((SUBLANE, tm), F32)
        for i, term in enumerate(_bf16_terms(2.0 ** (-8.0 * (h + 1) / N_HEADS) * LOG2E)):
            aug = jnp.where((row8 == i) | (row8 == i + SLOPE_TERMS), term, aug)
        full = jnp.concatenate([blk * r * gqT_ref[...], aug, zpad], axis=0)
        qaugT_ref[0, h * HEAD_PAD:(h + 1) * HEAD_PAD, :] = full.astype(BF16)

    kraw = proj(cols.k, N_HEADS * HEAD_DIM)
    posf = (j * tm + lax.broadcasted_iota(jnp.int32, (tm, HEAD_PAD), 0)).astype(F32)
    lane = lax.broadcasted_iota(jnp.int32, (tm, HEAD_PAD), 1)
    pos_hi = jnp.floor(posf * (1.0 / POS_SPLIT)) * POS_SPLIT
    in_hi = (lane >= HEAD_DIM) & (lane < HEAD_DIM + SLOPE_TERMS)
    in_lo = (lane >= HEAD_DIM + SLOPE_TERMS) & (lane < HEAD_DIM + 2 * SLOPE_TERMS)
    posmat = jnp.where(in_hi, pos_hi, jnp.where(in_lo, posf - pos_hi, 0.0))
    for h in range(N_HEADS):
        pair = kraw[:, (h // 2) * LANE:(h // 2 + 1) * LANE]
        if h % 2:
            pair = pltpu.roll(pair, HEAD_DIM, axis=1)
        kg = jnp.where(lane < HEAD_DIM, pair, 0.0)
        r = lax.rsqrt(jnp.sum(kg * kg, axis=-1, keepdims=True) * (1.0 / HEAD_DIM) + EPS)
        kaug_ref[0, :, h * HEAD_PAD:(h + 1) * HEAD_PAD] = (kg * r * gk_ref[...] + posmat).astype(BF16)

    vT_ref[0] = proj(cols.v, N_HEADS * HEAD_DIM).T.astype(BF16)
    qiT_ref[0] = proj(cols.qi, IDX_HEADS * IDX_DIM).T.astype(BF16)
    kw = proj(0, LANE, wkw_ref)
    kw_ref[0] = kw.astype(BF16)
    wiT_ref[0] = kw.T[IDX_DIM:IDX_DIM + IDX_HEADS, :]

    qm = proj(0, MEM_HEADS * MEM_HEAD_DIM, wqm_ref)
    mem_scale = MEM_HEAD_DIM ** -0.5
    heads = []
    for h in range(MEM_HEADS):
        sl = slice(h * MEM_HEAD_DIM, (h + 1) * MEM_HEAD_DIM)
        qh = _rms_rows(qm[:, sl], gmq_ref[...]).astype(BF16)
        lg = jnp.dot(qh, kmT_ref[0, sl, :], preferred_element_type=F32) * mem_scale
        p = jnp.exp(lg - jnp.max(lg, axis=-1, keepdims=True))
        denom = jnp.sum(p, axis=-1, keepdims=True)
        oh = jnp.dot(p.astype(BF16), vm_ref[0, :, sl], preferred_element_type=F32)
        heads.append(oh / denom)
    om = jnp.concatenate(heads, axis=1).astype(BF16)
    ym = jnp.dot(om, wmout_ref[...], preferred_element_type=F32)

    g0 = jax.nn.sigmoid(proj(0, d_model, wg_ref))
    g2 = jax.nn.sigmoid(proj(2 * d_model, d_model, wg_ref))
    part_ref[0] = (g0 * ya + g2 * ym).astype(BF16)
    g1_ref[0] = jax.nn.sigmoid(proj(d_model, d_model, wg_ref)).astype(BF16)


def _in_proj(x, g_mix, w_pieces, conv_a_w, w_a_out_bf, gk_pad, gqT, kmT, vm, mq_g, w_m_out_bf, tm):
    B, T, D = x.shape
    cols = _InCols(D)
    M = vm.shape[1]
    memw = MEM_HEADS * MEM_HEAD_DIM
    tok = lambda w: pl.BlockSpec((1, tm, w), lambda b, j: (b, j, 0))
    tokT = lambda w: pl.BlockSpec((1, w, tm), lambda b, j: (b, 0, j))
    return pl.pallas_call(
        functools.partial(_in_proj_kernel, cols, tm, D),
        grid=(B, T // tm),
        in_specs=[
            tok(D),
            _const_spec((1, D)),
            _const_spec((D, cols.main)),
            _const_spec((D, LANE)),
            _const_spec((D, memw)),
            _const_spec((D, N_BRANCH * D)),
            _const_spec((3, cols.sc)),
            _const_spec((cols.sc, D)),
            _const_spec((1, HEAD_PAD)),
            _const_spec((HEAD_DIM, tm)),
            pl.BlockSpec((1, memw, M), lambda b, j: (b, 0, 0)),
            pl.BlockSpec((1, M, memw), lambda b, j: (b, 0, 0)),
            _const_spec((1, MEM_HEAD_DIM)),
            _const_spec((memw, D)),
        ],
        out_specs=[
            tok(D), tok(D),
            tokT(N_HEADS * HEAD_PAD), tok(N_HEADS * HEAD_PAD),
            tokT(N_HEADS * HEAD_DIM), tokT(IDX_HEADS * IDX_DIM),
            tok(LANE), tokT(IDX_HEADS),
        ],
        out_shape=[
            jax.ShapeDtypeStruct((B, T, D), BF16),
            jax.ShapeDtypeStruct((B, T, D), BF16),
            jax.ShapeDtypeStruct((B, N_HEADS * HEAD_PAD, T), BF16),
            jax.ShapeDtypeStruct((B, T, N_HEADS * HEAD_PAD), BF16),
            jax.ShapeDtypeStruct((B, N_HEADS * HEAD_DIM, T), BF16),
            jax.ShapeDtypeStruct((B, IDX_HEADS * IDX_DIM, T), BF16),
            jax.ShapeDtypeStruct((B, T, LANE), BF16),
            jax.ShapeDtypeStruct((B, IDX_HEADS, T), F32),
        ],
        scratch_shapes=[pltpu.VMEM((SUBLANE, cols.sc), F32),
                        pltpu.VMEM((cols.sc // LANE, tm + SUBLANE, LANE), F32)],
        compiler_params=pltpu.CompilerParams(
            dimension_semantics=("arbitrary", "arbitrary"), vmem_limit_bytes=VMEM_LIMIT),
        name="in_proj",
    )(x, g_mix, *w_pieces, conv_a_w, w_a_out_bf, gk_pad, gqT, kmT, vm, mq_g, w_m_out_bf)


def _dsa_kernel(k_sel, kw_ref, qiT_ref, wiT_ref, kaug_ref, qaugT_ref, vT_ref, out_ref,
                s_ref, lg_ref, acc_ref):
    i = pl.program_id(1)
    n_tiles = i + 1
    qpos = i * QB + lax.broadcasted_iota(jnp.int32, (1, QB), 1)
    row = lax.broadcasted_iota(jnp.int32, (KT, QB), 0)
    groups = KT // SUBLANE
    inf = jnp.float32(jnp.inf)

    def tile_start(t):
        return pl.multiple_of(t * KT, KT)

    def for_tiles(body, init, group=2):
        def run(t0, count, carry):
            for j in range(count):
                carry = body(t0 + j, carry)
            return carry
        carry = lax.fori_loop(0, n_tiles // group, lambda p, c: run(group * p, group, c), init)
        done = (n_tiles // group) * group
        part = group // 2
        while part >= 1:
            take = (n_tiles & part) != 0
            carry = lax.cond(take, lambda c, d=done, n=part: run(d, n, c), lambda c: c, carry)
            done = done + jnp.where(take, part, 0)
            part //= 2
        return carry

    zrows = jnp.zeros((LANE - IDX_DIM, QB), BF16)

    def score_tile(t, carry):
        lo_part, hi_part = carry
        r0 = tile_start(t)
        kw_t = kw_ref[0, pl.ds(r0, KT), :]
        acc = jnp.zeros((KT, QB), F32)
        for h in range(IDX_HEADS):
            rhs = jnp.concatenate([qiT_ref[0, h * IDX_DIM:(h + 1) * IDX_DIM, :], zrows], axis=0)
            lg = jnp.dot(kw_t, rhs, preferred_element_type=F32)
            acc = acc + wiT_ref[0, h:h + 1, :] * jnp.maximum(lg, 0.0)
        valid = r0 + row <= qpos
        s_ref[pl.ds(r0, KT), :] = jnp.where(valid, acc, -inf)
        lo_t = jnp.min(jnp.where(valid, acc, inf).reshape(groups, SUBLANE, QB), axis=0)
        hi_t = jnp.max(jnp.where(valid, acc, -inf).reshape(groups, SUBLANE, QB), axis=0)
        return jnp.minimum(lo_part, lo_t), jnp.maximum(hi_part, hi_t)

    lo_part, hi_part = for_tiles(
        score_tile,
        (jnp.full((SUBLANE, QB), jnp.inf, F32), jnp.full((SUBLANE, QB), -jnp.inf, F32)), group=4)

    def tree(x3, op):
        while x3.shape[0] > 1:
            half = x3.shape[0] // 2
            x3 = op(x3[:half], x3[half:])
        return x3[0]

    def load3(t):
        return s_ref[pl.ds(tile_start(t), KT), :].reshape(groups, SUBLANE, QB)

    def bcast8(v):
        return jnp.broadcast_to(v, (SUBLANE, QB))[None]

    def count_ge(v):
        v8 = bcast8(v)

        def body(t, part):
            return part + tree(jnp.where(load3(t) >= v8, 1.0, 0.0), jnp.add)

        part = for_tiles(body, jnp.zeros((SUBLANE, QB), F32))
        return jnp.sum(part, axis=0, keepdims=True)

    def bracket_ends(lo, hi):
        lo8, hi8 = bcast8(lo), bcast8(hi)

        def body(t, carry):
            s3 = load3(t)
            a_t = tree(jnp.where(s3 >= lo8, s3, inf), jnp.minimum)
            b_t = tree(jnp.where(s3 < hi8, s3, -inf), jnp.maximum)
            return jnp.minimum(carry[0], a_t), jnp.maximum(carry[1], b_t)

        a_part, b_part = for_tiles(
            body, (jnp.full((SUBLANE, QB), jnp.inf, F32), jnp.full((SUBLANE, QB), -jnp.inf, F32)))
        return jnp.min(a_part, axis=0, keepdims=True), jnp.max(b_part, axis=0, keepdims=True)

    kf = jnp.minimum(qpos + 1, k_sel).astype(F32)
    top = jnp.max(hi_part, axis=0, keepdims=True)
    lo0 = jnp.min(lo_part, axis=0, keepdims=True)
    hi0 = top + jnp.maximum(jnp.abs(top), 1.0) * 1e-3
    clo0 = (qpos + 1).astype(F32)

    def split(st, mid):
        lo, hi, clo, chi = st
        c = count_ge(mid)
        ge = c >= kf
        return (jnp.where(ge, mid, lo), jnp.where(ge, hi, mid),
                jnp.where(ge, c, clo), jnp.where(ge, chi, c))

    def bisect(_, st):
        lo, hi = st[0], st[1]
        mid = lo + (hi - lo) * 0.5
        return split(st, jnp.where(mid <= lo, hi, mid))

    st = lax.fori_loop(0, BISECT_FIXED, bisect, (lo0, hi0, clo0, jnp.zeros((1, QB), F32)))

    def unresolved(a, b):
        return jnp.max(jnp.where(a < b, 1.0, 0.0))

    def refine(carry):
        st, a, b = carry[:4], carry[4], carry[5]
        mid = a + (b - a) * 0.5
        st = split(st, jnp.where(mid <= a, b, mid))
        a, b = bracket_ends(st[0], st[1])
        return st + (a, b, unresolved(a, b))

    a0, b0 = bracket_ends(st[0], st[1])
    lo, hi, clo, cgt, tau, _, _ = lax.while_loop(
        lambda c: c[6] > 0.0, refine, st + (a0, b0, unresolved(a0, b0)))

    need = kf - cgt

    @pl.when(jnp.max(jnp.where(clo > kf, 1.0, 0.0)) > 0.0)
    def _():
        tri = jnp.where(row >= lax.broadcasted_iota(jnp.int32, (KT, KT), 1), 1.0, 0.0).astype(BF16)

        def drop_tile(t, before):
            r0 = tile_start(t)
            s_t = s_ref[pl.ds(r0, KT), :]
            tied = jnp.where(s_t == tau, 1.0, 0.0)
            rank = before + jnp.dot(tri, tied.astype(BF16), preferred_element_type=F32)
            s_ref[pl.ds(r0, KT), :] = jnp.where(tied * rank > need, -inf, s_t)
            return before + jnp.sum(tied, axis=0, keepdims=True)

        lax.fori_loop(0, n_tiles, drop_tile, jnp.zeros((1, QB), F32))

    def logits_tile(t, mparts):
        r0 = tile_start(t)
        bias = jnp.where(s_ref[pl.ds(r0, KT), :] >= tau, 0.0, -inf)
        out = []
        for h in range(N_HEADS):
            hs = slice(h * HEAD_PAD, (h + 1) * HEAD_PAD)
            s = jnp.dot(kaug_ref[0, pl.ds(r0, KT), hs], qaugT_ref[0, hs, :],
                        preferred_element_type=F32) + bias
            lg_ref[h, pl.ds(r0, KT), :] = s
            out.append(jnp.maximum(mparts[h], jnp.max(s.reshape(groups, SUBLANE, QB), axis=0)))
        return tuple(out)

    mparts = for_tiles(logits_tile,
                       tuple(jnp.full((SUBLANE, QB), NEG_BIG, F32) for _ in range(N_HEADS)), group=4)
    m_rows = [jnp.max(mp, axis=0, keepdims=True) for mp in mparts]

    acc_ref[...] = jnp.zeros(acc_ref.shape, F32)

    ones_rows = jnp.ones((2 * SUBLANE, KT), BF16)

    def pv_tile(t, carry):
        r0 = tile_start(t)
        for h in range(N_HEADS):
            vs = slice(h * HEAD_DIM, (h + 1) * HEAD_DIM)
            ws = slice(h * VROWS, (h + 1) * VROWS)
            p = jnp.exp2((lg_ref[h, pl.ds(r0, KT), :] - m_rows[h]).astype(BF16))
            v_ones = jnp.concatenate([vT_ref[0, vs, pl.ds(r0, KT)], ones_rows], axis=0)
            acc_ref[ws, :] += jnp.dot(v_ones, p, preferred_element_type=F32)
        return carry

    for_tiles(pv_tile, 0, group=4)

    for h in range(N_HEADS):
        vs = slice(h * HEAD_DIM, (h + 1) * HEAD_DIM)
        denom = acc_ref[h * VROWS + HEAD_DIM:h * VROWS + HEAD_DIM + 1, :]
        out_ref[0, vs, :] = (acc_ref[h * VROWS:h * VROWS + HEAD_DIM, :] / denom).astype(BF16)


def _dsa(kw, qiT, wiT, kaug, qaugT, vT, k_sel):
    B, T, _ = kaug.shape
    assert T % QB == 0 and QB == KT
    att = N_HEADS * HEAD_DIM
    qblk = lambda w: pl.BlockSpec((1, w, QB), lambda b, i: (b, 0, i))
    return pl.pallas_call(
        functools.partial(_dsa_kernel, k_sel),
        grid=(B, T // QB),
        in_specs=[
            pl.BlockSpec((1, T, LANE), lambda b, i: (b, 0, 0)),
            qblk(IDX_HEADS * IDX_DIM),
            qblk(IDX_HEADS),
            pl.BlockSpec((1, T, N_HEADS * HEAD_PAD), lambda b, i: (b, 0, 0)),
            qblk(N_HEADS * HEAD_PAD),
            pl.BlockSpec((1, att, T), lambda b, i: (b, 0, 0)),
        ],
        out_specs=qblk(att),
        out_shape=jax.ShapeDtypeStruct((B, att, T), BF16),
        scratch_shapes=[
            pltpu.VMEM((T, QB), F32),
            pltpu.VMEM((N_HEADS, T, QB), F32),
            pltpu.VMEM((N_HEADS * VROWS, QB), F32),
        ],
        compiler_params=pltpu.CompilerParams(
            dimension_semantics=("arbitrary", "arbitrary"), vmem_limit_bytes=VMEM_LIMIT),
        name="dsa",
    )(kw, qiT, wiT, kaug, qaugT, vT)


def _out_ffn_kernel(tm, d_ff, x_ref, part_ref, g1_ref, attnT_ref, wb_ref, wo_ref, gffn_ref,
                    wup_ref, convf_ref, wdown_ref, out_ref, halo_ref, cbuf_ref, hmid_ref):
    j = pl.program_id(1)
    yb = lax.dot_general(attnT_ref[0], wb_ref[...], (((0,), (0,)), ((), ())),
                         preferred_element_type=F32)
    merged = (part_ref[0].astype(F32) + g1_ref[0].astype(F32) * yb).astype(BF16)
    x1 = x_ref[0] + jnp.dot(merged, wo_ref[...], preferred_element_type=F32)
    xn2 = _rms_rows(x1, gffn_ref[...]).astype(BF16)

    @pl.when(j == 0)
    def _():
        halo_ref[...] = jnp.zeros_like(halo_ref)

    def up_proj(c):
        return tuple(jnp.dot(xn2, wup_ref[:, off:off + FF_CHUNK], preferred_element_type=F32)
                     for off in (c * FF_CHUNK, d_ff + c * FF_CHUNK))

    def conv_chunk(up, off, slot):
        cs = slice(off, off + FF_CHUNK)
        out = _causal_conv3(up, halo_ref[:, cs], convf_ref[:, cs], cbuf_ref, slot)
        halo_ref[:, cs] = up[tm - SUBLANE:tm, :]
        return out

    n_chunks = d_ff // FF_CHUNK
    blocks = FF_CHUNK // LANE
    ups = up_proj(0)
    for c in range(n_chunks):
        nxt = up_proj(c + 1) if c + 1 < n_chunks else None
        slot = (c % FF_SLOTS) * 2 * blocks
        gate = conv_chunk(ups[0], c * FF_CHUNK, slot)
        val = conv_chunk(ups[1], d_ff + c * FF_CHUNK, slot + blocks)
        hmid_ref[:, c * FF_CHUNK:(c + 1) * FF_CHUNK] = (gate * jax.nn.sigmoid(gate) * val).astype(BF16)
        ups = nxt
    out_ref[0] = x1 + jnp.dot(hmid_ref[...], wdown_ref[...], preferred_element_type=F32)


def _out_ffn(x, part, g1, attnT, w_b_bf, w_o_bf, g_ffn, w_up_bf, conv_f_w, w_down_bf, tm):
    B, T, D = x.shape
    d_ff = w_down_bf.shape[0]
    att = N_HEADS * HEAD_DIM
    assert d_ff % FF_CHUNK == 0
    tok = lambda w: pl.BlockSpec((1, tm, w), lambda b, j: (b, j, 0))
    return pl.pallas_call(
        functools.partial(_out_ffn_kernel, tm, d_ff),
        grid=(B, T // tm),
        in_specs=[
            tok(D), tok(D), tok(D),
            pl.BlockSpec((1, att, tm), lambda b, j: (b, 0, j)),
            _const_spec((att, D)),
            _const_spec((D, D)),
            _const_spec((1, D)),
            _const_spec((D, 2 * d_ff)),
            _const_spec((3, 2 * d_ff)),
            _const_spec((d_ff, D)),
        ],
        out_specs=tok(D),
        out_shape=jax.ShapeDtypeStruct((B, T, D), F32),
        scratch_shapes=[pltpu.VMEM((SUBLANE, 2 * d_ff), F32),
                        pltpu.VMEM((FF_SLOTS * 2 * FF_CHUNK // LANE, tm + SUBLANE, LANE), F32),
                        pltpu.VMEM((tm, d_ff), BF16)],
        compiler_params=pltpu.CompilerParams(
            dimension_semantics=("arbitrary", "arbitrary"), vmem_limit_bytes=VMEM_LIMIT),
        name="out_ffn",
    )(x, part, g1, attnT, w_b_bf, w_o_bf, g_ffn, w_up_bf, conv_f_w, w_down_bf)


def _layer(x, mem, g_mix, w_in, conv_a_w, w_a_out, q_norm_g, k_norm_g, w_b_out, g_mem, w_mem_kv,
           mq_norm_g, mk_norm_g, w_m_out, w_o, g_ffn, w_up, conv_f_w, w_down):
    B, T, D = x.shape
    tm_in = min(TM_IN, T)
    tm_out = min(TM_OUT, T)
    k_sel = min(TOPK_MAX, T // 4)
    row = lambda g: g.reshape(1, -1).astype(F32)

    kmT, vm = _mem_kv(mem, row(g_mem), w_mem_kv.astype(BF16), row(mk_norm_g))

    scale = HEAD_DIM ** -0.5
    gk_pad = jnp.pad(row(k_norm_g), ((0, 0), (0, HEAD_PAD - HEAD_DIM)))
    gqT = jnp.broadcast_to((q_norm_g.astype(F32) * (scale * LOG2E))[:, None], (HEAD_DIM, tm_in))
    part, g1, qaugT, kaug, vT, qiT, kw, wiT = _in_proj(
        x, row(g_mix), _split_w_in(*w_in), conv_a_w, w_a_out.astype(BF16), gk_pad, gqT,
        kmT, vm, row(mq_norm_g), w_m_out.astype(BF16), tm_in)

    attnT = _dsa(kw, qiT, wiT, kaug, qaugT, vT, k_sel)

    return _out_ffn(x, part, g1, attnT, w_b_out.astype(BF16), w_o.astype(BF16), row(g_ffn),
                    w_up.astype(BF16), conv_f_w, w_down.astype(BF16), tm_out)


def kernel(x, mem, g_mix, w_in, conv_a_w, w_a_out, q_norm_g, k_norm_g, w_b_out, g_mem, w_mem_kv,
           mq_norm_g, mk_norm_g, w_m_out, w_o, g_ffn, w_up, conv_f_w, w_down):
    for l in range(g_mix.shape[0]):
        x = _layer(x, mem, g_mix[l], (w_in, l), conv_a_w[l], w_a_out[l], q_norm_g[l], k_norm_g[l],
                   w_b_out[l], g_mem[l], w_mem_kv[l], mq_norm_g[l], mk_norm_g[l], w_m_out[l],
                   w_o[l], g_ffn[l], w_up[l], conv_f_w[l], w_down[l])
    return x
```

```python
import functools

import jax
import jax.numpy as jnp
import numpy as np
from jax import lax
from jax.experimental import pallas as pl
from jax.experimental.pallas import tpu as pltpu

F32 = jnp.float32
BF16 = jnp.bfloat16

EPS = 1e-6
N_HEADS = 8
HEAD_DIM = 64
IDX_HEADS = 8
IDX_DIM = 32
MEM_HEADS = 4
MEM_HEAD_DIM = 128
TOPK_MAX = 256
N_BRANCH = 3

LANE = 128
SUBLANE = 8
HEAD_PAD = LANE
POS_SPLIT = 8.0
LOG2E = 1.4426950408889634
SLOPE_TERMS = 3
VMEM_LIMIT = 60 * 1024 * 1024

TM_IN = 512
W_COLS = 256
TM_OUT = 512
QB = 256
KT = 256
FF_CHUNK = 256
FF_SLOTS = 4
VROWS = HEAD_DIM + 16
INTERP_STEPS = 8
BISECT_STEPS = 5
INTERP_CLAMP = 0.2
NEG_BIG = -1e30


def _const_spec(shape):
    nd = len(shape)
    return pl.BlockSpec(shape, lambda *_: (0,) * nd, pipeline_mode=pl.Buffered(1))


def _causal_conv3(u, prev8, w3, buf_ref, slot):
    rows = u.shape[0]
    outs = []
    for c in range(u.shape[1] // LANE):
        cs = slice(c * LANE, (c + 1) * LANE)
        buf_ref[slot + c, 0:SUBLANE, :] = prev8[:, cs]
        buf_ref[slot + c, SUBLANE:SUBLANE + rows, :] = u[:, cs]
        u1 = buf_ref[slot + c, pl.ds(SUBLANE - 1, rows), :]
        u2 = buf_ref[slot + c, pl.ds(SUBLANE - 2, rows), :]
        outs.append(w3[0:1, cs] * u2 + w3[1:2, cs] * u1 + w3[2:3, cs] * u[:, cs])
    return jnp.concatenate(outs, axis=1)


def _bf16_terms(c):
    rest, out = np.float32(c), []
    for _ in range(SLOPE_TERMS):
        term = np.float32(np.asarray(rest, dtype=jnp.bfloat16))
        out.append(float(term))
        rest = np.float32(rest - term)
    return out


def _rms_rows(x, g):
    return x * lax.rsqrt(jnp.mean(x * x, axis=-1, keepdims=True) + EPS) * g


def _mem_kv_kernel(mem_ref, gmem_ref, wkv_ref, gmk_ref, kmT_ref, vm_ref):
    memn = _rms_rows(mem_ref[0], gmem_ref[...]).astype(BF16)
    width = MEM_HEADS * MEM_HEAD_DIM
    km = jnp.dot(memn, wkv_ref[:, 0:width], preferred_element_type=F32)
    vm = jnp.dot(memn, wkv_ref[:, width:2 * width], preferred_element_type=F32)
    heads = []
    for h in range(MEM_HEADS):
        kh = km[:, h * MEM_HEAD_DIM:(h + 1) * MEM_HEAD_DIM]
        heads.append(_rms_rows(kh, gmk_ref[...]))
    kmT_ref[0] = jnp.concatenate(heads, axis=1).T.astype(BF16)
    vm_ref[0] = vm.astype(BF16)


def _mem_kv(mem, g_mem, w_kv_bf, mk_g):
    B, M, D = mem.shape
    width = MEM_HEADS * MEM_HEAD_DIM
    return pl.pallas_call(
        _mem_kv_kernel,
        grid=(B,),
        in_specs=[
            pl.BlockSpec((1, M, D), lambda b: (b, 0, 0)),
            _const_spec((1, D)),
            _const_spec((D, 2 * width)),
            _const_spec((1, MEM_HEAD_DIM)),
        ],
        out_specs=[
            pl.BlockSpec((1, width, M), lambda b: (b, 0, 0)),
            pl.BlockSpec((1, M, width), lambda b: (b, 0, 0)),
        ],
        out_shape=[
            jax.ShapeDtypeStruct((B, width, M), BF16),
            jax.ShapeDtypeStruct((B, M, width), BF16),
        ],
        compiler_params=pltpu.CompilerParams(
            dimension_semantics=("arbitrary",), vmem_limit_bytes=VMEM_LIMIT),
        name="mem_kv",
    )(mem, g_mem, w_kv_bf, mk_g)


class _InCols:
    def __init__(self, d_model):
        sc = d_model // 2
        self.sc = sc
        self.bch = 0
        self.q = self.bch + 3 * sc
        self.k = self.q + N_HEADS * HEAD_DIM
        self.v = self.k + N_HEADS * HEAD_DIM
        self.qi = self.v + N_HEADS * HEAD_DIM
        self.main = self.qi + IDX_HEADS * IDX_DIM


def _w_piece_kernel(rows, w_ref, out_ref):
    w = w_ref[0]
    if rows < out_ref.shape[1]:
        w = jnp.concatenate([w, jnp.zeros((out_ref.shape[1] - rows, w.shape[1]), F32)], axis=0)
    out_ref[...] = w.T.astype(BF16)


def _w_piece(w_t, layer, col0, cols, out_cols):
    D = w_t.shape[2]
    step = min(cols, W_COLS)
    assert cols % step == 0 and col0 % SUBLANE == 0 and (cols == step or out_cols == cols)
    return pl.pallas_call(
        functools.partial(_w_piece_kernel, step),
        grid=(cols // step,),
        in_specs=[pl.BlockSpec((pl.Element(1), pl.Element(step), pl.Element(D)),
                               lambda c: (layer, pl.multiple_of(col0 + c * step, SUBLANE), 0))],
        out_specs=pl.BlockSpec((D, out_cols // (cols // step)), lambda c: (0, c)),
        out_shape=jax.ShapeDtypeStruct((D, out_cols), BF16),
        compiler_params=pltpu.CompilerParams(
            dimension_semantics=("arbitrary",), vmem_limit_bytes=VMEM_LIMIT),
        name="w_piece",
    )(w_t)


def _split_w_in(w_in_stack, layer):
    _, D, width = w_in_stack.shape
    cols = _InCols(D)
    kw_cols = IDX_DIM + IDX_HEADS
    qm_cols = MEM_HEADS * MEM_HEAD_DIM
    assert cols.main + kw_cols + qm_cols + N_BRANCH * D == width
    w_t = jnp.swapaxes(w_in_stack, 1, 2)
    return (_w_piece(w_t, layer, 0, cols.main, cols.main),
            _w_piece(w_t, layer, cols.main, kw_cols, LANE),
            _w_piece(w_t, layer, cols.main + kw_cols, qm_cols, qm_cols),
            _w_piece(w_t, layer, cols.main + kw_cols + qm_cols, N_BRANCH * D, N_BRANCH * D))


def _in_proj_kernel(cols, tm, d_model,
                    x_ref, gmix_ref, w_ref, wkw_ref, wqm_ref, wg_ref, convw_ref, waout_ref, gk_ref, gqT_ref,
                    kmT_ref, vm_ref, gmq_ref, wmout_ref,
                    part_ref, g1_ref, qaugT_ref, kaug_ref, vT_ref, qiT_ref, kw_ref, wiT_ref,
                    halo_ref, cbuf_ref):
    j = pl.program_id(1)
    sc = cols.sc
    xn = _rms_rows(x_ref[0], gmix_ref[...]).astype(BF16)

    def proj(off, width, ref=w_ref):
        return jnp.dot(xn, ref[:, off:off + width], preferred_element_type=F32)

    @pl.when(j == 0)
    def _():
        halo_ref[...] = jnp.zeros_like(halo_ref)

    ch = proj(cols.bch + sc, sc) * proj(cols.bch + 2 * sc, sc)
    conv = _causal_conv3(ch, halo_ref[...], convw_ref[...], cbuf_ref, 0)
    halo_ref[...] = ch[tm - SUBLANE:tm, :]
    ua = (proj(cols.bch, sc) * conv).astype(BF16)
    ya = jnp.dot(ua, waout_ref[...], preferred_element_type=F32)

    qT = proj(cols.q, N_HEADS * HEAD_DIM).T
    row8 = lax.broadcasted_iota(jnp.int32, (SUBLANE, tm), 0)
    zpad = jnp.zeros((HEAD_PAD - HEAD_DIM - SUBLANE, tm), F32)
    for h in range(N_HEADS):
        blk = qT[h * HEAD_DIM:(h + 1) * HEAD_DIM, :]
        r = lax.rsqrt(jnp.sum(blk * blk, axis=0, keepdims=True) * (1.0 / HEAD_DIM) + EPS)
        aug = jnp.zeros((SUBLANE, tm), F32)
        for i, term in enumerate(_bf16_terms(2.0 ** (-8.0 * (h + 1) / N_HEADS) * LOG2E)):
            aug = jnp.where((row8 == i) | (row8 == i + SLOPE_TERMS), term, aug)
        full = jnp.concatenate([blk * r * gqT_ref[...], aug, zpad], axis=0)
        qaugT_ref[0, h * HEAD_PAD:(h + 1) * HEAD_PAD, :] = full.astype(BF16)

    kraw = proj(cols.k, N_HEADS * HEAD_DIM)
    posf = (j * tm + lax.broadcasted_iota(jnp.int32, (tm, HEAD_PAD), 0)).astype(F32)
    lane = lax.broadcasted_iota(jnp.int32, (tm, HEAD_PAD), 1)
    pos_hi = jnp.floor(posf * (1.0 / POS_SPLIT)) * POS_SPLIT
    in_hi = (lane >= HEAD_DIM) & (lane < HEAD_DIM + SLOPE_TERMS)
    in_lo = (lane >= HEAD_DIM + SLOPE_TERMS) & (lane < HEAD_DIM + 2 * SLOPE_TERMS)
    posmat = jnp.where(in_hi, pos_hi, jnp.where(in_lo, posf - pos_hi, 0.0))
    for h in range(N_HEADS):
        pair = kraw[:, (h // 2) * LANE:(h // 2 + 1) * LANE]
        if h % 2:
            pair = pltpu.roll(pair, HEAD_DIM, axis=1)
        kg = jnp.where(lane < HEAD_DIM, pair, 0.0)
        r = lax.rsqrt(jnp.sum(kg * kg, axis=-1, keepdims=True) * (1.0 / HEAD_DIM) + EPS)
        kaug_ref[0, :, h * HEAD_PAD:(h + 1) * HEAD_PAD] = (kg * r * gk_ref[...] + posmat).astype(BF16)

    vT_ref[0] = proj(cols.v, N_HEADS * HEAD_DIM).T.astype(BF16)
    qiT_ref[0] = proj(cols.qi, IDX_HEADS * IDX_DIM).T.astype(BF16)
    qm = proj(0, MEM_HEADS * MEM_HEAD_DIM, wqm_ref)
    mem_scale = MEM_HEAD_DIM ** -0.5
    heads = []
    for h in range(MEM_HEADS):
        sl = slice(h * MEM_HEAD_DIM, (h + 1) * MEM_HEAD_DIM)
        qh = _rms_rows(qm[:, sl], gmq_ref[...]).astype(BF16)
        lg = jnp.dot(qh, kmT_ref[0, sl, :], preferred_element_type=F32) * mem_scale
        p = jnp.exp(lg - jnp.max(lg, axis=-1, keepdims=True))
        denom = jnp.sum(p, axis=-1, keepdims=True)
        oh = jnp.dot(p.astype(BF16), vm_ref[0, :, sl], preferred_element_type=F32)
        heads.append(oh / denom)
    om = jnp.concatenate(heads, axis=1).astype(BF16)
    ym = jnp.dot(om, wmout_ref[...], preferred_element_type=F32)

    g1_ref[0] = jax.nn.sigmoid(proj(d_model, d_model, wg_ref)).astype(BF16)
    g0 = jax.nn.sigmoid(proj(0, d_model, wg_ref))
    g2 = jax.nn.sigmoid(proj(2 * d_model, d_model, wg_ref))
    part_ref[0] = (g0 * ya + g2 * ym).astype(BF16)

    kw = proj(0, LANE, wkw_ref)
    kw_ref[0] = kw.astype(BF16)
    wiT_ref[0] = kw.T[IDX_DIM:IDX_DIM + IDX_HEADS, :]


def _in_proj(x, g_mix, w_pieces, conv_a_w, w_a_out_bf, gk_pad, gqT, kmT, vm, mq_g, w_m_out_bf, tm):
    B, T, D = x.shape
    cols = _InCols(D)
    M = vm.shape[1]
    memw = MEM_HEADS * MEM_HEAD_DIM
    tok = lambda w: pl.BlockSpec((1, tm, w), lambda b, j: (b, j, 0))
    tokT = lambda w: pl.BlockSpec((1, w, tm), lambda b, j: (b, 0, j))
    return pl.pallas_call(
        functools.partial(_in_proj_kernel, cols, tm, D),
        grid=(B, T // tm),
        in_specs=[
            tok(D),
            _const_spec((1, D)),
            _const_spec((D, cols.main)),
            _const_spec((D, LANE)),
            _const_spec((D, memw)),
            _const_spec((D, N_BRANCH * D)),
            _const_spec((3, cols.sc)),
            _const_spec((cols.sc, D)),
            _const_spec((1, HEAD_PAD)),
            _const_spec((HEAD_DIM, tm)),
            pl.BlockSpec((1, memw, M), lambda b, j: (b, 0, 0)),
            pl.BlockSpec((1, M, memw), lambda b, j: (b, 0, 0)),
            _const_spec((1, MEM_HEAD_DIM)),
            _const_spec((memw, D)),
        ],
        out_specs=[
            tok(D), tok(D),
            tokT(N_HEADS * HEAD_PAD), tok(N_HEADS * HEAD_PAD),
            tokT(N_HEADS * HEAD_DIM), tokT(IDX_HEADS * IDX_DIM),
            tok(LANE), tokT(IDX_HEADS),
        ],
        out_shape=[
            jax.ShapeDtypeStruct((B, T, D), BF16),
            jax.ShapeDtypeStruct((B, T, D), BF16),
            jax.ShapeDtypeStruct((B, N_HEADS * HEAD_PAD, T), BF16),
            jax.ShapeDtypeStruct((B, T, N_HEADS * HEAD_PAD), BF16),
            jax.ShapeDtypeStruct((B, N_HEADS * HEAD_DIM, T), BF16),
            jax.ShapeDtypeStruct((B, IDX_HEADS * IDX_DIM, T), BF16),
            jax.ShapeDtypeStruct((B, T, LANE), BF16),
            jax.ShapeDtypeStruct((B, IDX_HEADS, T), F32),
        ],
        scratch_shapes=[pltpu.VMEM((SUBLANE, cols.sc), F32),
                        pltpu.VMEM((cols.sc // LANE, tm + SUBLANE, LANE), F32)],
        compiler_params=pltpu.CompilerParams(
            dimension_semantics=("arbitrary", "arbitrary"), vmem_limit_bytes=VMEM_LIMIT),
        name="in_proj",
    )(x, g_mix, *w_pieces, conv_a_w, w_a_out_bf, gk_pad, gqT, kmT, vm, mq_g, w_m_out_bf)


def _dsa_kernel(k_sel, kw_ref, qiT_ref, wiT_ref, kaug_ref, qaugT_ref, vT_ref, out_ref,
                s_ref, lg_ref, acc_ref):
    i = pl.program_id(1)
    n_tiles = i + 1
    qpos = i * QB + lax.broadcasted_iota(jnp.int32, (1, QB), 1)
    row = lax.broadcasted_iota(jnp.int32, (KT, QB), 0)
    groups = KT // SUBLANE
    inf = jnp.float32(jnp.inf)

    def tile_start(t):
        return pl.multiple_of(t * KT, KT)

    def for_tiles(body, init, group=2):
        def run(t0, count, carry):
            for j in range(count):
                carry = body(t0 + j, carry)
            return carry
        carry = lax.fori_loop(0, n_tiles // group, lambda p, c: run(group * p, group, c), init)
        done = (n_tiles // group) * group
        part = group // 2
        while part >= 1:
            take = (n_tiles & part) != 0
            carry = lax.cond(take, lambda c, d=done, n=part: run(d, n, c), lambda c: c, carry)
            done = done + jnp.where(take, part, 0)
            part //= 2
        return carry

    zrows = jnp.zeros((LANE - IDX_DIM, QB), BF16)

    def score_tile(t, carry):
        lo_part, hi_part = carry
        r0 = tile_start(t)
        kw_t = kw_ref[0, pl.ds(r0, KT), :]
        acc = jnp.zeros((KT, QB), F32)
        for h in range(IDX_HEADS):
            rhs = jnp.concatenate([qiT_ref[0, h * IDX_DIM:(h + 1) * IDX_DIM, :], zrows], axis=0)
            lg = jnp.dot(kw_t, rhs, preferred_element_type=F32)
            acc = acc + wiT_ref[0, h:h + 1, :] * jnp.maximum(lg, 0.0)
        valid = r0 + row <= qpos
        s_ref[pl.ds(r0, KT), :] = jnp.where(valid, acc, -inf)
        lo_t = jnp.min(jnp.where(valid, acc, inf).reshape(groups, SUBLANE, QB), axis=0)
        hi_t = jnp.max(jnp.where(valid, acc, -inf).reshape(groups, SUBLANE, QB), axis=0)
        return jnp.minimum(lo_part, lo_t), jnp.maximum(hi_part, hi_t)

    lo_part, hi_part = for_tiles(
        score_tile,
        (jnp.full((SUBLANE, QB), jnp.inf, F32), jnp.full((SUBLANE, QB), -jnp.inf, F32)), group=4)

    def tree(x3, op):
        while x3.shape[0] > 1:
            half = x3.shape[0] // 2
            x3 = op(x3[:half], x3[half:])
        return x3[0]

    def load3(t):
        return s_ref[pl.ds(tile_start(t), KT), :].reshape(groups, SUBLANE, QB)

    def bcast8(v):
        return jnp.broadcast_to(v, (SUBLANE, QB))[None]

    def count_ge(v):
        v8 = bcast8(v)

        def body(t, part):
            return part + tree(jnp.where(load3(t) >= v8, 1.0, 0.0), jnp.add)

        part = for_tiles(body, jnp.zeros((SUBLANE, QB), F32))
        return jnp.sum(part, axis=0, keepdims=True)

    def bracket_ends(lo, hi):
        lo8, hi8 = bcast8(lo), bcast8(hi)

        def body(t, carry):
            s3 = load3(t)
            a_t = tree(jnp.where(s3 >= lo8, s3, inf), jnp.minimum)
            b_t = tree(jnp.where(s3 < hi8, s3, -inf), jnp.maximum)
            return jnp.minimum(carry[0], a_t), jnp.maximum(carry[1], b_t)

        a_part, b_part = for_tiles(
            body, (jnp.full((SUBLANE, QB), jnp.inf, F32), jnp.full((SUBLANE, QB), -jnp.inf, F32)))
        return jnp.min(a_part, axis=0, keepdims=True), jnp.max(b_part, axis=0, keepdims=True)

    kf = jnp.minimum(qpos + 1, k_sel).astype(F32)
    top = jnp.max(hi_part, axis=0, keepdims=True)
    lo0 = jnp.min(lo_part, axis=0, keepdims=True)
    hi0 = top + jnp.maximum(jnp.abs(top), 1.0) * 1e-3
    clo0 = (qpos + 1).astype(F32)

    def split(st, mid):
        lo, hi, clo, chi = st
        c = count_ge(mid)
        ge = c >= kf
        return (jnp.where(ge, mid, lo), jnp.where(ge, hi, mid),
                jnp.where(ge, c, clo), jnp.where(ge, chi, c))

    def interpolate(_, st):
        lo, hi, clo, chi = st
        frac = jnp.clip((clo - kf + 0.5) / jnp.maximum(clo - chi, 1.0), INTERP_CLAMP, 1.0 - INTERP_CLAMP)
        mid = lo + (hi - lo) * frac
        return split(st, jnp.where(mid <= lo, hi, mid))

    def bisect(_, st):
        lo, hi = st[0], st[1]
        mid = lo + (hi - lo) * 0.5
        return split(st, jnp.where(mid <= lo, hi, mid))

    st = lax.fori_loop(0, INTERP_STEPS, interpolate, (lo0, hi0, clo0, jnp.zeros((1, QB), F32)))
    st = lax.fori_loop(0, BISECT_STEPS, bisect, st)

    def unresolved(a, b):
        return jnp.max(jnp.where(a < b, 1.0, 0.0))

    def refine(carry):
        st, a, b = carry[:4], carry[4], carry[5]
        mid = a + (b - a) * 0.5
        st = split(st, jnp.where(mid <= a, b, mid))
        a, b = bracket_ends(st[0], st[1])
        return st + (a, b, unresolved(a, b))

    a0, b0 = bracket_ends(st[0], st[1])
    lo, hi, clo, cgt, tau, _, _ = lax.while_loop(
        lambda c: c[6] > 0.0, refine, st + (a0, b0, unresolved(a0, b0)))

    need = kf - cgt

    @pl.when(jnp.max(jnp.where(clo > kf, 1.0, 0.0)) > 0.0)
    def _():
        tri = jnp.where(row >= lax.broadcasted_iota(jnp.int32, (KT, KT), 1), 1.0, 0.0).astype(BF16)

        def drop_tile(t, before):
            r0 = tile_start(t)
            s_t = s_ref[pl.ds(r0, KT), :]
            tied = jnp.where(s_t == tau, 1.0, 0.0)
            rank = before + jnp.dot(tri, tied.astype(BF16), preferred_element_type=F32)
            s_ref[pl.ds(r0, KT), :] = jnp.where(tied * rank > need, -inf, s_t)
            return before + jnp.sum(tied, axis=0, keepdims=True)

        lax.fori_loop(0, n_tiles, drop_tile, jnp.zeros((1, QB), F32))

    def logits_tile(t, mparts):
        r0 = tile_start(t)
        bias = jnp.where(s_ref[pl.ds(r0, KT), :] >= tau, 0.0, -inf)
        out = []
        for h in range(N_HEADS):
            hs = slice(h * HEAD_PAD, (h + 1) * HEAD_PAD)
            s = jnp.dot(kaug_ref[0, pl.ds(r0, KT), hs], qaugT_ref[0, hs, :],
                        preferred_element_type=F32) + bias
            lg_ref[h, pl.ds(r0, KT), :] = s
            out.append(jnp.maximum(mparts[h], jnp.max(s.reshape(groups, SUBLANE, QB), axis=0)))
        return tuple(out)

    mparts = for_tiles(logits_tile,
                       tuple(jnp.full((SUBLANE, QB), NEG_BIG, F32) for _ in range(N_HEADS)), group=4)
    m_rows = [jnp.max(mp, axis=0, keepdims=True) for mp in mparts]

    acc_ref[...] = jnp.zeros(acc_ref.shape, F32)

    ones_rows = jnp.ones((2 * SUBLANE, KT), BF16)

    def pv_tile(t, carry):
        r0 = tile_start(t)
        for h in range(N_HEADS):
            vs = slice(h * HEAD_DIM, (h + 1) * HEAD_DIM)
            ws = slice(h * VROWS, (h + 1) * VROWS)
            p = jnp.exp2((lg_ref[h, pl.ds(r0, KT), :] - m_rows[h]).astype(BF16))
            v_ones = jnp.concatenate([vT_ref[0, vs, pl.ds(r0, KT)], ones_rows], axis=0)
            acc_ref[ws, :] += jnp.dot(v_ones, p, preferred_element_type=F32)
        return carry

    for_tiles(pv_tile, 0, group=4)

    for h in range(N_HEADS):
        vs = slice(h * HEAD_DIM, (h + 1) * HEAD_DIM)
        denom = acc_ref[h * VROWS + HEAD_DIM:h * VROWS + HEAD_DIM + 1, :]
        out_ref[0, vs, :] = (acc_ref[h * VROWS:h * VROWS + HEAD_DIM, :] / denom).astype(BF16)


def _dsa(kw, qiT, wiT, kaug, qaugT, vT, k_sel):
    B, T, _ = kaug.shape
    assert T % QB == 0 and QB == KT
    att = N_HEADS * HEAD_DIM
    qblk = lambda w: pl.BlockSpec((1, w, QB), lambda b, i: (b, 0, i))
    return pl.pallas_call(
        functools.partial(_dsa_kernel, k_sel),
        grid=(B, T // QB),
        in_specs=[
            pl.BlockSpec((1, T, LANE), lambda b, i: (b, 0, 0)),
            qblk(IDX_HEADS * IDX_DIM),
            qblk(IDX_HEADS),
            pl.BlockSpec((1, T, N_HEADS * HEAD_PAD), lambda b, i: (b, 0, 0)),
            qblk(N_HEADS * HEAD_PAD),
            pl.BlockSpec((1, att, T), lambda b, i: (b, 0, 0)),
        ],
        out_specs=qblk(att),
        out_shape=jax.ShapeDtypeStruct((B, att, T), BF16),
        scratch_shapes=[
            pltpu.VMEM((T, QB), F32),
            pltpu.VMEM((N_HEADS, T, QB), F32),
            pltpu.VMEM((N_HEADS * VROWS, QB), F32),
        ],
        compiler_params=pltpu.CompilerParams(
            dimension_semantics=("arbitrary", "arbitrary"), vmem_limit_bytes=VMEM_LIMIT),
        name="dsa",
    )(kw, qiT, wiT, kaug, qaugT, vT)


def _out_ffn_kernel(tm, d_ff, x_ref, part_ref, g1_ref, attnT_ref, wb_ref, wo_ref, gffn_ref,
                    wup_ref, convf_ref, wdown_ref, out_ref, halo_ref, cbuf_ref, hmid_ref):
    j = pl.program_id(1)
    yb = lax.dot_general(attnT_ref[0], wb_ref[...], (((0,), (0,)), ((), ())),
                         preferred_element_type=F32)
    merged = (part_ref[0].astype(F32) + g1_ref[0].astype(F32) * yb).astype(BF16)
    x1 = x_ref[0] + jnp.dot(merged, wo_ref[...], preferred_element_type=F32)
    xn2 = _rms_rows(x1, gffn_ref[...]).astype(BF16)

    @pl.when(j == 0)
    def _():
        halo_ref[...] = jnp.zeros_like(halo_ref)

    def up_proj(c):
        return tuple(jnp.dot(xn2, wup_ref[:, off:off + FF_CHUNK], preferred_element_type=F32)
                     for off in (c * FF_CHUNK, d_ff + c * FF_CHUNK))

    def conv_chunk(up, off, slot):
        cs = slice(off, off + FF_CHUNK)
        out = _causal_conv3(up, halo_ref[:, cs], convf_ref[:, cs], cbuf_ref, slot)
        halo_ref[:, cs] = up[tm - SUBLANE:tm, :]
        return out

    n_chunks = d_ff // FF_CHUNK
    blocks = FF_CHUNK // LANE
    ups = up_proj(0)
    for c in range(n_chunks):
        nxt = up_proj(c + 1) if c + 1 < n_chunks else None
        slot = (c % FF_SLOTS) * 2 * blocks
        gate = conv_chunk(ups[0], c * FF_CHUNK, slot)
        val = conv_chunk(ups[1], d_ff + c * FF_CHUNK, slot + blocks)
        hmid_ref[:, c * FF_CHUNK:(c + 1) * FF_CHUNK] = (gate * jax.nn.sigmoid(gate) * val).astype(BF16)
        ups = nxt
    out_ref[0] = x1 + jnp.dot(hmid_ref[...], wdown_ref[...], preferred_element_type=F32)


def _out_ffn(x, part, g1, attnT, w_b_bf, w_o_bf, g_ffn, w_up_bf, conv_f_w, w_down_bf, tm):
    B, T, D = x.shape
    d_ff = w_down_bf.shape[0]
    att = N_HEADS * HEAD_DIM
    assert d_ff % FF_CHUNK == 0
    tok = lambda w: pl.BlockSpec((1, tm, w), lambda b, j: (b, j, 0))
    return pl.pallas_call(
        functools.partial(_out_ffn_kernel, tm, d_ff),
        grid=(B, T // tm),
        in_specs=[
            tok(D), tok(D), tok(D),
            pl.BlockSpec((1, att, tm), lambda b, j: (b, 0, j)),
            _const_spec((att, D)),
            _const_spec((D, D)),
            _const_spec((1, D)),
            _const_spec((D, 2 * d_ff)),
            _const_spec((3, 2 * d_ff)),
            _const_spec((d_ff, D)),
        ],
        out_specs=tok(D),
        out_shape=jax.ShapeDtypeStruct((B, T, D), F32),
        scratch_shapes=[pltpu.VMEM((SUBLANE, 2 * d_ff), F32),
                        pltpu.VMEM((FF_SLOTS * 2 * FF_CHUNK // LANE, tm + SUBLANE, LANE), F32),
                        pltpu.VMEM((tm, d_ff), BF16)],
        compiler_params=pltpu.CompilerParams(
            dimension_semantics=("arbitrary", "arbitrary"), vmem_limit_bytes=VMEM_LIMIT),
        name="out_ffn",
    )(x, part, g1, attnT, w_b_bf, w_o_bf, g_ffn, w_up_bf, conv_f_w, w_down_bf)


def _layer(x, mem, g_mix, w_in, conv_a_w, w_a_out, q_norm_g, k_norm_g, w_b_out, g_mem, w_mem_kv,
           mq_norm_g, mk_norm_g, w_m_out, w_o, g_ffn, w_up, conv_f_w, w_down):
    B, T, D = x.shape
    tm_in = min(TM_IN, T)
    tm_out = min(TM_OUT, T)
    k_sel = min(TOPK_MAX, T // 4)
    assert T <= 256 * POS_SPLIT, "key positions must split into two bf16-exact parts"
    row = lambda g: g.reshape(1, -1).astype(F32)

    kmT, vm = _mem_kv(mem, row(g_mem), w_mem_kv.astype(BF16), row(mk_norm_g))

    scale = HEAD_DIM ** -0.5
    gk_pad = jnp.pad(row(k_norm_g), ((0, 0), (0, HEAD_PAD - HEAD_DIM)))
    gqT = jnp.broadcast_to((q_norm_g.astype(F32) * (scale * LOG2E))[:, None], (HEAD_DIM, tm_in))
    part, g1, qaugT, kaug, vT, qiT, kw, wiT = _in_proj(
        x, row(g_mix), _split_w_in(*w_in), conv_a_w, w_a_out.astype(BF16), gk_pad, gqT,
        kmT, vm, row(mq_norm_g), w_m_out.astype(BF16), tm_in)

    attnT = _dsa(kw, qiT, wiT, kaug, qaugT, vT, k_sel)

    return _out_ffn(x, part, g1, attnT, w_b_out.astype(BF16), w_o.astype(BF16), row(g_ffn),
                    w_up.astype(BF16), conv_f_w, w_down.astype(BF16), tm_out)


def kernel(x, mem, g_mix, w_in, conv_a_w, w_a_out, q_norm_g, k_norm_g, w_b_out, g_mem, w_mem_kv,
           mq_norm_g, mk_norm_g, w_m_out, w_o, g_ffn, w_up, conv_f_w, w_down):
    for l in range(g_mix.shape[0]):
        x = _layer(x, mem, g_mix[l], (w_in, l), conv_a_w[l], w_a_out[l], q_norm_g[l], k_norm_g[l],
                   w_b_out[l], g_mem[l], w_mem_kv[l], mq_norm_g[l], mk_norm_g[l], w_m_out[l],
                   w_o[l], g_ffn[l], w_up[l], conv_f_w[l], w_down[l])
    return x
```

```python
import functools

import jax
import jax.numpy as jnp
import numpy as np
from jax import lax
from jax.experimental import pallas as pl
from jax.experimental.pallas import tpu as pltpu

F32 = jnp.float32
BF16 = jnp.bfloat16

EPS = 1e-6
N_HEADS = 8
HEAD_DIM = 64
IDX_HEADS = 8
IDX_DIM = 32
MEM_HEADS = 4
MEM_HEAD_DIM = 128
TOPK_MAX = 256
N_BRANCH = 3

LANE = 128
SUBLANE = 8
HEAD_PAD = LANE
POS_SPLIT = 8.0
LOG2E = 1.4426950408889634
SLOPE_TERMS = 3
VMEM_LIMIT = 60 * 1024 * 1024

TM_IN = 512
W_COLS = 256
TM_OUT = 512
QB = 256
KT = 256
FF_CHUNK = 256
FF_SLOTS = 4
VROWS = HEAD_DIM + 16
INTERP_STEPS = 8
BISECT_STEPS = 5
INTERP_CLAMP = 0.2
NEG_BIG = -1e30


def _const_spec(shape):
    nd = len(shape)
    return pl.BlockSpec(shape, lambda *_: (0,) * nd, pipeline_mode=pl.Buffered(1))


def _causal_conv3(u, prev8, w3, buf_ref, slot):
    rows = u.shape[0]
    outs = []
    for c in range(u.shape[1] // LANE):
        cs = slice(c * LANE, (c + 1) * LANE)
        buf_ref[slot + c, 0:SUBLANE, :] = prev8[:, cs]
        buf_ref[slot + c, SUBLANE:SUBLANE + rows, :] = u[:, cs]
        u1 = buf_ref[slot + c, pl.ds(SUBLANE - 1, rows), :]
        u2 = buf_ref[slot + c, pl.ds(SUBLANE - 2, rows), :]
        outs.append(w3[0:1, cs] * u2 + w3[1:2, cs] * u1 + w3[2:3, cs] * u[:, cs])
    return jnp.concatenate(outs, axis=1)


def _bf16_terms(c):
    rest, out = np.float32(c), []
    for _ in range(SLOPE_TERMS):
        term = np.float32(np.asarray(rest, dtype=jnp.bfloat16))
        out.append(float(term))
        rest = np.float32(rest - term)
    return out


def _rms_rows(x, g):
    return x * lax.rsqrt(jnp.mean(x * x, axis=-1, keepdims=True) + EPS) * g


def _mem_kv_kernel(mem_ref, gmem_ref, wkv_ref, gmk_ref, kmT_ref, vm_ref):
    memn = _rms_rows(mem_ref[0], gmem_ref[...]).astype(BF16)
    width = MEM_HEADS * MEM_HEAD_DIM
    km = jnp.dot(memn, wkv_ref[:, 0:width], preferred_element_type=F32)
    vm = jnp.dot(memn, wkv_ref[:, width:2 * width], preferred_element_type=F32)
    heads = []
    for h in range(MEM_HEADS):
        kh = km[:, h * MEM_HEAD_DIM:(h + 1) * MEM_HEAD_DIM]
        heads.append(_rms_rows(kh, gmk_ref[...]))
    kmT_ref[0] = jnp.concatenate(heads, axis=1).T.astype(BF16)
    vm_ref[0] = vm.astype(BF16)


def _mem_kv(mem, g_mem, w_kv_bf, mk_g):
    B, M, D = mem.shape
    width = MEM_HEADS * MEM_HEAD_DIM
    return pl.pallas_call(
        _mem_kv_kernel,
        grid=(B,),
        in_specs=[
            pl.BlockSpec((1, M, D), lambda b: (b, 0, 0)),
            _const_spec((1, D)),
            _const_spec((D, 2 * width)),
            _const_spec((1, MEM_HEAD_DIM)),
        ],
        out_specs=[
            pl.BlockSpec((1, width, M), lambda b: (b, 0, 0)),
            pl.BlockSpec((1, M, width), lambda b: (b, 0, 0)),
        ],
        out_shape=[
            jax.ShapeDtypeStruct((B, width, M), BF16),
            jax.ShapeDtypeStruct((B, M, width), BF16),
        ],
        compiler_params=pltpu.CompilerParams(
            dimension_semantics=("arbitrary",), vmem_limit_bytes=VMEM_LIMIT),
        name="mem_kv",
    )(mem, g_mem, w_kv_bf, mk_g)


class _InCols:
    def __init__(self, d_model):
        sc = d_model // 2
        self.sc = sc
        self.bch = 0
        self.q = self.bch + 3 * sc
        self.k = self.q + N_HEADS * HEAD_DIM
        self.v = self.k + N_HEADS * HEAD_DIM
        self.qi = self.v + N_HEADS * HEAD_DIM
        self.main = self.qi + IDX_HEADS * IDX_DIM


def _w_piece_kernel(rows, w_ref, out_ref):
    w = w_ref[0]
    if rows < out_ref.shape[1]:
        w = jnp.concatenate([w, jnp.zeros((out_ref.shape[1] - rows, w.shape[1]), F32)], axis=0)
    out_ref[...] = w.T.astype(BF16)


def _w_piece(w_t, layer, col0, cols, out_cols):
    D = w_t.shape[2]
    step = min(cols, W_COLS)
    assert cols % step == 0 and col0 % SUBLANE == 0 and (cols == step or out_cols == cols)
    return pl.pallas_call(
        functools.partial(_w_piece_kernel, step),
        grid=(cols // step,),
        in_specs=[pl.BlockSpec((pl.Element(1), pl.Element(step), pl.Element(D)),
                               lambda c: (layer, pl.multiple_of(col0 + c * step, SUBLANE), 0))],
        out_specs=pl.BlockSpec((D, out_cols // (cols // step)), lambda c: (0, c)),
        out_shape=jax.ShapeDtypeStruct((D, out_cols), BF16),
        compiler_params=pltpu.CompilerParams(
            dimension_semantics=("arbitrary",), vmem_limit_bytes=VMEM_LIMIT),
        name="w_piece",
    )(w_t)


def _split_w_in(w_in_stack, layer):
    _, D, width = w_in_stack.shape
    cols = _InCols(D)
    kw_cols = IDX_DIM + IDX_HEADS
    qm_cols = MEM_HEADS * MEM_HEAD_DIM
    assert cols.main + kw_cols + qm_cols + N_BRANCH * D == width
    w_t = jnp.swapaxes(w_in_stack, 1, 2)
    return (_w_piece(w_t, layer, 0, cols.main, cols.main),
            _w_piece(w_t, layer, cols.main, kw_cols, LANE),
            _w_piece(w_t, layer, cols.main + kw_cols, qm_cols, qm_cols),
            _w_piece(w_t, layer, cols.main + kw_cols + qm_cols, N_BRANCH * D, N_BRANCH * D))


def _in_proj_kernel(cols, tm, d_model,
                    x_ref, gmix_ref, w_ref, wkw_ref, wqm_ref, wg_ref, convw_ref, waout_ref, gk_ref, gqT_ref,
                    kmT_ref, vm_ref, gmq_ref, wmout_ref,
                    part_ref, g1_ref, qaugT_ref, kaug_ref, vT_ref, qiT_ref, kw_ref, wiT_ref,
                    halo_ref, cbuf_ref):
    j = pl.program_id(1)
    sc = cols.sc
    xn = _rms_rows(x_ref[0], gmix_ref[...]).astype(BF16)

    def proj(off, width, ref=w_ref):
        return jnp.dot(xn, ref[:, off:off + width], preferred_element_type=F32)

    @pl.when(j == 0)
    def _():
        halo_ref[...] = jnp.zeros_like(halo_ref)

    ch = proj(cols.bch + sc, sc) * proj(cols.bch + 2 * sc, sc)
    conv = _causal_conv3(ch, halo_ref[...], convw_ref[...], cbuf_ref, 0)
    halo_ref[...] = ch[tm - SUBLANE:tm, :]
    ua = (proj(cols.bch, sc) * conv).astype(BF16)
    ya = jnp.dot(ua, waout_ref[...], preferred_element_type=F32)

    qT = proj(cols.q, N_HEADS * HEAD_DIM).T
    row8 = lax.broadcasted_iota(jnp.int32, (SUBLANE, tm), 0)
    zpad = jnp.zeros((HEAD_PAD - HEAD_DIM - SUBLANE, tm), F32)
    for h in range(N_HEADS):
        blk = qT[h * HEAD_DIM:(h + 1) * HEAD_DIM, :]
        r = lax.rsqrt(jnp.sum(blk * blk, axis=0, keepdims=True) * (1.0 / HEAD_DIM) + EPS)
        aug = jnp.zeros((SUBLANE, tm), F32)
        for i, term in enumerate(_bf16_terms(2.0 ** (-8.0 * (h + 1) / N_HEADS) * LOG2E)):
            aug = jnp.where((row8 == i) | (row8 == i + SLOPE_TERMS), term, aug)
        full = jnp.concatenate([blk * r * gqT_ref[...], aug, zpad], axis=0)
        qaugT_ref[0, h * HEAD_PAD:(h + 1) * HEAD_PAD, :] = full.astype(BF16)

    kraw = proj(cols.k, N_HEADS * HEAD_DIM)
    posf = (j * tm + lax.broadcasted_iota(jnp.int32, (tm, HEAD_PAD), 0)).astype(F32)
    lane = lax.broadcasted_iota(jnp.int32, (tm, HEAD_PAD), 1)
    pos_hi = jnp.floor(posf * (1.0 / POS_SPLIT)) * POS_SPLIT
    in_hi = (lane >= HEAD_DIM) & (lane < HEAD_DIM + SLOPE_TERMS)
    in_lo = (lane >= HEAD_DIM + SLOPE_TERMS) & (lane < HEAD_DIM + 2 * SLOPE_TERMS)
    posmat = jnp.where(in_hi, pos_hi, jnp.where(in_lo, posf - pos_hi, 0.0))
    for h in range(N_HEADS):
        pair = kraw[:, (h // 2) * LANE:(h // 2 + 1) * LANE]
        if h % 2:
            pair = pltpu.roll(pair, HEAD_DIM, axis=1)
        kg = jnp.where(lane < HEAD_DIM, pair, 0.0)
        r = lax.rsqrt(jnp.sum(kg * kg, axis=-1, keepdims=True) * (1.0 / HEAD_DIM) + EPS)
        kaug_ref[0, :, h * HEAD_PAD:(h + 1) * HEAD_PAD] = (kg * r * gk_ref[...] + posmat).astype(BF16)

    vT_ref[0] = proj(cols.v, N_HEADS * HEAD_DIM).T.astype(BF16)
    qiT_ref[0] = proj(cols.qi, IDX_HEADS * IDX_DIM).T.astype(BF16)
    qm = proj(0, MEM_HEADS * MEM_HEAD_DIM, wqm_ref)
    mem_scale = MEM_HEAD_DIM ** -0.5
    heads = []
    for h in range(MEM_HEADS):
        sl = slice(h * MEM_HEAD_DIM, (h + 1) * MEM_HEAD_DIM)
        qh = _rms_rows(qm[:, sl], gmq_ref[...]).astype(BF16)
        lg = jnp.dot(qh, kmT_ref[0, sl, :], preferred_element_type=F32) * mem_scale
        p = jnp.exp(lg - jnp.max(lg, axis=-1, keepdims=True))
        denom = jnp.sum(p, axis=-1, keepdims=True)
        oh = jnp.dot(p.astype(BF16), vm_ref[0, :, sl], preferred_element_type=F32)
        heads.append(oh / denom)
    om = jnp.concatenate(heads, axis=1).astype(BF16)
    ym = jnp.dot(om, wmout_ref[...], preferred_element_type=F32)

    g1_ref[0] = jax.nn.sigmoid(proj(d_model, d_model, wg_ref)).astype(BF16)
    g0 = jax.nn.sigmoid(proj(0, d_model, wg_ref))
    g2 = jax.nn.sigmoid(proj(2 * d_model, d_model, wg_ref))
    part_ref[0] = (g0 * ya + g2 * ym).astype(BF16)

    kw = proj(0, LANE, wkw_ref)
    kw_ref[0] = kw.astype(BF16)
    wiT_ref[0] = kw.T[IDX_DIM:IDX_DIM + IDX_HEADS, :]


def _in_proj(x, g_mix, w_pieces, conv_a_w, w_a_out_bf, gk_pad, gqT, kmT, vm, mq_g, w_m_out_bf, tm):
    B, T, D = x.shape
    cols = _InCols(D)
    M = vm.shape[1]
    memw = MEM_HEADS * MEM_HEAD_DIM
    tok = lambda w: pl.BlockSpec((1, tm, w), lambda b, j: (b, j, 0))
    tokT = lambda w: pl.BlockSpec((1, w, tm), lambda b, j: (b, 0, j))
    return pl.pallas_call(
        functools.partial(_in_proj_kernel, cols, tm, D),
        grid=(B, T // tm),
        in_specs=[
            tok(D),
            _const_spec((1, D)),
            _const_spec((D, cols.main)),
            _const_spec((D, LANE)),
            _const_spec((D, memw)),
            _const_spec((D, N_BRANCH * D)),
            _const_spec((3, cols.sc)),
            _const_spec((cols.sc, D)),
            _const_spec((1, HEAD_PAD)),
            _const_spec((HEAD_DIM, tm)),
            pl.BlockSpec((1, memw, M), lambda b, j: (b, 0, 0)),
            pl.BlockSpec((1, M, memw), lambda b, j: (b, 0, 0)),
            _const_spec((1, MEM_HEAD_DIM)),
            _const_spec((memw, D)),
        ],
        out_specs=[
            tok(D), tok(D),
            tokT(N_HEADS * HEAD_PAD), tok(N_HEADS * HEAD_PAD),
            tokT(N_HEADS * HEAD_DIM), tokT(IDX_HEADS * IDX_DIM),
            tok(LANE), tokT(IDX_HEADS),
        ],
        out_shape=[
            jax.ShapeDtypeStruct((B, T, D), BF16),
            jax.ShapeDtypeStruct((B, T, D), BF16),
            jax.ShapeDtypeStruct((B, N_HEADS * HEAD_PAD, T), BF16),
            jax.ShapeDtypeStruct((B, T, N_HEADS * HEAD_PAD), BF16),
            jax.ShapeDtypeStruct((B, N_HEADS * HEAD_DIM, T), BF16),
            jax.ShapeDtypeStruct((B, IDX_HEADS * IDX_DIM, T), BF16),
            jax.ShapeDtypeStruct((B, T, LANE), BF16),
            jax.ShapeDtypeStruct((B, IDX_HEADS, T), F32),
        ],
        scratch_shapes=[pltpu.VMEM((SUBLANE, cols.sc), F32),
                        pltpu.VMEM((cols.sc // LANE, tm + SUBLANE, LANE), F32)],
        compiler_params=pltpu.CompilerParams(
            dimension_semantics=("arbitrary", "arbitrary"), vmem_limit_bytes=VMEM_LIMIT),
        name="in_proj",
    )(x, g_mix, *w_pieces, conv_a_w, w_a_out_bf, gk_pad, gqT, kmT, vm, mq_g, w_m_out_bf)


def _dsa_kernel(k_sel, kw_ref, qiT_ref, wiT_ref, kaug_ref, qaugT_ref, vT_ref, out_ref,
                s_ref, lg_ref, acc_ref):
    i = pl.program_id(1)
    n_tiles = i + 1
    qpos = i * QB + lax.broadcasted_iota(jnp.int32, (1, QB), 1)
    row = lax.broadcasted_iota(jnp.int32, (KT, QB), 0)
    groups = KT // SUBLANE
    inf = jnp.float32(jnp.inf)

    def tile_start(t):
        return pl.multiple_of(t * KT, KT)

    def for_tiles(body, init, group=2):
        def run(t0, count, carry):
            for j in range(count):
                carry = body(t0 + j, carry)
            return carry
        carry = lax.fori_loop(0, n_tiles // group, lambda p, c: run(group * p, group, c), init)
        done = (n_tiles // group) * group
        part = group // 2
        while part >= 1:
            take = (n_tiles & part) != 0
            carry = lax.cond(take, lambda c, d=done, n=part: run(d, n, c), lambda c: c, carry)
            done = done + jnp.where(take, part, 0)
            part //= 2
        return carry

    zrows = jnp.zeros((LANE - IDX_DIM, QB), BF16)

    def score_tile(t, carry):
        lo_part, hi_part = carry
        r0 = tile_start(t)
        kw_t = kw_ref[0, pl.ds(r0, KT), :]
        acc = jnp.zeros((KT, QB), F32)
        for h in range(IDX_HEADS):
            rhs = jnp.concatenate([qiT_ref[0, h * IDX_DIM:(h + 1) * IDX_DIM, :], zrows], axis=0)
            lg = jnp.dot(kw_t, rhs, preferred_element_type=F32)
            acc = acc + wiT_ref[0, h:h + 1, :] * jnp.maximum(lg, 0.0)
        valid = r0 + row <= qpos
        s_ref[pl.ds(r0, KT), :] = jnp.where(valid, acc, -inf)
        lo_t = jnp.min(jnp.where(valid, acc, inf).reshape(groups, SUBLANE, QB), axis=0)
        hi_t = jnp.max(jnp.where(valid, acc, -inf).reshape(groups, SUBLANE, QB), axis=0)
        return jnp.minimum(lo_part, lo_t), jnp.maximum(hi_part, hi_t)

    lo_part, hi_part = for_tiles(
        score_tile,
        (jnp.full((SUBLANE, QB), jnp.inf, F32), jnp.full((SUBLANE, QB), -jnp.inf, F32)), group=4)

    def tree(x3, op):
        while x3.shape[0] > 1:
            half = x3.shape[0] // 2
            x3 = op(x3[:half], x3[half:])
        return x3[0]

    def load3(t):
        return s_ref[pl.ds(tile_start(t), KT), :].reshape(groups, SUBLANE, QB)

    def bcast8(v):
        return jnp.broadcast_to(v, (SUBLANE, QB))[None]

    def count_ge(v):
        v8 = bcast8(v)

        def body(t, part):
            return part + tree(jnp.where(load3(t) >= v8, 1.0, 0.0), jnp.add)

        part = for_tiles(body, jnp.zeros((SUBLANE, QB), F32))
        return jnp.sum(part, axis=0, keepdims=True)

    def bracket_ends(lo, hi):
        lo8, hi8 = bcast8(lo), bcast8(hi)

        def body(t, carry):
            s3 = load3(t)
            a_t = tree(jnp.where(s3 >= lo8, s3, inf), jnp.minimum)
            b_t = tree(jnp.where(s3 < hi8, s3, -inf), jnp.maximum)
            return jnp.minimum(carry[0], a_t), jnp.maximum(carry[1], b_t)

        a_part, b_part = for_tiles(
            body, (jnp.full((SUBLANE, QB), jnp.inf, F32), jnp.full((SUBLANE, QB), -jnp.inf, F32)))
        return jnp.min(a_part, axis=0, keepdims=True), jnp.max(b_part, axis=0, keepdims=True)

    kf = jnp.minimum(qpos + 1, k_sel).astype(F32)
    top = jnp.max(hi_part, axis=0, keepdims=True)
    lo0 = jnp.min(lo_part, axis=0, keepdims=True)
    hi0 = top + jnp.maximum(jnp.abs(top), 1.0) * 1e-3
    clo0 = (qpos + 1).astype(F32)

    def split(st, mid):
        lo, hi, clo, chi = st
        c = count_ge(mid)
        ge = c >= kf
        return (jnp.where(ge, mid, lo), jnp.where(ge, hi, mid),
                jnp.where(ge, c, clo), jnp.where(ge, chi, c))

    def interpolate(_, st):
        lo, hi, clo, chi = st
        frac = jnp.clip((clo - kf + 0.5) / jnp.maximum(clo - chi, 1.0), INTERP_CLAMP, 1.0 - INTERP_CLAMP)
        mid = lo + (hi - lo) * frac
        return split(st, jnp.where(mid <= lo, hi, mid))

    def bisect(_, st):
        lo, hi = st[0], st[1]
        mid = lo + (hi - lo) * 0.5
        return split(st, jnp.where(mid <= lo, hi, mid))

    st = lax.fori_loop(0, INTERP_STEPS, interpolate, (lo0, hi0, clo0, jnp.zeros((1, QB), F32)))
    st = lax.fori_loop(0, BISECT_STEPS, bisect, st)

    def unresolved(a, b):
        return jnp.max(jnp.where(a < b, 1.0, 0.0))

    def refine(carry):
        (lo, hi, clo, chi), a, b = carry[:4], carry[4], carry[5]
        mid = a + (b - a) * 0.5
        mid = jnp.where(mid <= a, b, mid)
        mid8 = bcast8(mid)

        def body(t, parts):
            s3 = load3(t)
            ge = s3 >= mid8
            return (parts[0] + tree(jnp.where(ge, 1.0, 0.0), jnp.add),
                    jnp.minimum(parts[1], tree(jnp.where(ge, s3, inf), jnp.minimum)),
                    jnp.maximum(parts[2], tree(jnp.where(ge, -inf, s3), jnp.maximum)))

        parts = for_tiles(body, (jnp.zeros((SUBLANE, QB), F32), jnp.full((SUBLANE, QB), jnp.inf, F32),
                                 jnp.full((SUBLANE, QB), -jnp.inf, F32)))
        c = jnp.sum(parts[0], axis=0, keepdims=True)
        above = jnp.min(parts[1], axis=0, keepdims=True)
        below = jnp.max(parts[2], axis=0, keepdims=True)
        ge = c >= kf
        a, b = jnp.where(ge, above, a), jnp.where(ge, b, below)
        return (jnp.where(ge, mid, lo), jnp.where(ge, hi, mid), jnp.where(ge, c, clo), jnp.where(ge, chi, c),
                a, b, unresolved(a, b))

    a0, b0 = bracket_ends(st[0], st[1])
    lo, hi, clo, cgt, tau, _, _ = lax.while_loop(
        lambda c: c[6] > 0.0, refine, st + (a0, b0, unresolved(a0, b0)))

    need = kf - cgt

    @pl.when(jnp.max(jnp.where(clo > kf, 1.0, 0.0)) > 0.0)
    def _():
        tri = jnp.where(row >= lax.broadcasted_iota(jnp.int32, (KT, KT), 1), 1.0, 0.0).astype(BF16)

        def drop_tile(t, before):
            r0 = tile_start(t)
            s_t = s_ref[pl.ds(r0, KT), :]
            tied = jnp.where(s_t == tau, 1.0, 0.0)
            rank = before + jnp.dot(tri, tied.astype(BF16), preferred_element_type=F32)
            s_ref[pl.ds(r0, KT), :] = jnp.where(tied * rank > need, -inf, s_t)
            return before + jnp.sum(tied, axis=0, keepdims=True)

        lax.fori_loop(0, n_tiles, drop_tile, jnp.zeros((1, QB), F32))

    def logits_tile(t, mparts):
        r0 = tile_start(t)
        bias = jnp.where(s_ref[pl.ds(r0, KT), :] >= tau, 0.0, -inf)
        out = []
        for h in range(N_HEADS):
            hs = slice(h * HEAD_PAD, (h + 1) * HEAD_PAD)
            s = jnp.dot(kaug_ref[0, pl.ds(r0, KT), hs], qaugT_ref[0, hs, :],
                        preferred_element_type=F32) + bias
            lg_ref[h, pl.ds(r0, KT), :] = s
            out.append(jnp.maximum(mparts[h], jnp.max(s.reshape(groups, SUBLANE, QB), axis=0)))
        return tuple(out)

    mparts = for_tiles(logits_tile,
                       tuple(jnp.full((SUBLANE, QB), NEG_BIG, F32) for _ in range(N_HEADS)), group=4)
    m_rows = [jnp.max(mp, axis=0, keepdims=True) for mp in mparts]

    acc_ref[...] = jnp.zeros(acc_ref.shape, F32)

    ones_rows = jnp.ones((2 * SUBLANE, KT), BF16)

    def pv_tile(t, carry):
        r0 = tile_start(t)
        for h in range(N_HEADS):
            vs = slice(h * HEAD_DIM, (h + 1) * HEAD_DIM)
            ws = slice(h * VROWS, (h + 1) * VROWS)
            p = jnp.exp2((lg_ref[h, pl.ds(r0, KT), :] - m_rows[h]).astype(BF16))
            v_ones = jnp.concatenate([vT_ref[0, vs, pl.ds(r0, KT)], ones_rows], axis=0)
            acc_ref[ws, :] += jnp.dot(v_ones, p, preferred_element_type=F32)
        return carry

    for_tiles(pv_tile, 0, group=4)

    for h in range(N_HEADS):
        vs = slice(h * HEAD_DIM, (h + 1) * HEAD_DIM)
        denom = acc_ref[h * VROWS + HEAD_DIM:h * VROWS + HEAD_DIM + 1, :]
        out_ref[0, vs, :] = (acc_ref[h * VROWS:h * VROWS + HEAD_DIM, :] / denom).astype(BF16)


def _dsa(kw, qiT, wiT, kaug, qaugT, vT, k_sel):
    B, T, _ = kaug.shape
    assert T % QB == 0 and QB == KT
    att = N_HEADS * HEAD_DIM
    qblk = lambda w: pl.BlockSpec((1, w, QB), lambda b, i: (b, 0, i))
    return pl.pallas_call(
        functools.partial(_dsa_kernel, k_sel),
        grid=(B, T // QB),
        in_specs=[
            pl.BlockSpec((1, T, LANE), lambda b, i: (b, 0, 0)),
            qblk(IDX_HEADS * IDX_DIM),
            qblk(IDX_HEADS),
            pl.BlockSpec((1, T, N_HEADS * HEAD_PAD), lambda b, i: (b, 0, 0)),
            qblk(N_HEADS * HEAD_PAD),
            pl.BlockSpec((1, att, T), lambda b, i: (b, 0, 0)),
        ],
        out_specs=qblk(att),
        out_shape=jax.ShapeDtypeStruct((B, att, T), BF16),
        scratch_shapes=[
            pltpu.VMEM((T, QB), F32),
            pltpu.VMEM((N_HEADS, T, QB), F32),
            pltpu.VMEM((N_HEADS * VROWS, QB), F32),
        ],
        compiler_params=pltpu.CompilerParams(
            dimension_semantics=("arbitrary", "arbitrary"), vmem_limit_bytes=VMEM_LIMIT),
        name="dsa",
    )(kw, qiT, wiT, kaug, qaugT, vT)


def _out_ffn_kernel(tm, d_ff, x_ref, part_ref, g1_ref, attnT_ref, wb_ref, wo_ref, gffn_ref,
                    wup_ref, convf_ref, wdown_ref, out_ref, halo_ref, cbuf_ref, hmid_ref):
    j = pl.program_id(1)
    yb = lax.dot_general(attnT_ref[0], wb_ref[...], (((0,), (0,)), ((), ())),
                         preferred_element_type=F32)
    merged = (part_ref[0].astype(F32) + g1_ref[0].astype(F32) * yb).astype(BF16)
    x1 = x_ref[0] + jnp.dot(merged, wo_ref[...], preferred_element_type=F32)
    xn2 = _rms_rows(x1, gffn_ref[...]).astype(BF16)

    @pl.when(j == 0)
    def _():
        halo_ref[...] = jnp.zeros_like(halo_ref)

    def up_proj(c):
        return tuple(jnp.dot(xn2, wup_ref[:, off:off + FF_CHUNK], preferred_element_type=F32)
                     for off in (c * FF_CHUNK, d_ff + c * FF_CHUNK))

    def conv_chunk(up, off, slot):
        cs = slice(off, off + FF_CHUNK)
        out = _causal_conv3(up, halo_ref[:, cs], convf_ref[:, cs], cbuf_ref, slot)
        halo_ref[:, cs] = up[tm - SUBLANE:tm, :]
        return out

    n_chunks = d_ff // FF_CHUNK
    blocks = FF_CHUNK // LANE
    ups = up_proj(0)
    for c in range(n_chunks):
        nxt = up_proj(c + 1) if c + 1 < n_chunks else None
        slot = (c % FF_SLOTS) * 2 * blocks
        gate = conv_chunk(ups[0], c * FF_CHUNK, slot)
        val = conv_chunk(ups[1], d_ff + c * FF_CHUNK, slot + blocks)
        hmid_ref[:, c * FF_CHUNK:(c + 1) * FF_CHUNK] = (gate * jax.nn.sigmoid(gate) * val).astype(BF16)
        ups = nxt
    out_ref[0] = x1 + jnp.dot(hmid_ref[...], wdown_ref[...], preferred_element_type=F32)


def _out_ffn(x, part, g1, attnT, w_b_bf, w_o_bf, g_ffn, w_up_bf, conv_f_w, w_down_bf, tm):
    B, T, D = x.shape
    d_ff = w_down_bf.shape[0]
    att = N_HEADS * HEAD_DIM
    assert d_ff % FF_CHUNK == 0
    tok = lambda w: pl.BlockSpec((1, tm, w), lambda b, j: (b, j, 0))
    return pl.pallas_call(
        functools.partial(_out_ffn_kernel, tm, d_ff),
        grid=(B, T // tm),
        in_specs=[
            tok(D), tok(D), tok(D),
            pl.BlockSpec((1, att, tm), lambda b, j: (b, 0, j)),
            _const_spec((att, D)),
            _const_spec((D, D)),
            _const_spec((1, D)),
            _const_spec((D, 2 * d_ff)),
            _const_spec((3, 2 * d_ff)),
            _const_spec((d_ff, D)),
        ],
        out_specs=tok(D),
        out_shape=jax.ShapeDtypeStruct((B, T, D), F32),
        scratch_shapes=[pltpu.VMEM((SUBLANE, 2 * d_ff), F32),
                        pltpu.VMEM((FF_SLOTS * 2 * FF_CHUNK // LANE, tm + SUBLANE, LANE), F32),
                        pltpu.VMEM((tm, d_ff), BF16)],
        compiler_params=pltpu.CompilerParams(
            dimension_semantics=("arbitrary", "arbitrary"), vmem_limit_bytes=VMEM_LIMIT),
        name="out_ffn",
    )(x, part, g1, attnT, w_b_bf, w_o_bf, g_ffn, w_up_bf, conv_f_w, w_down_bf)


def _layer(x, mem, g_mix, w_in, conv_a_w, w_a_out, q_norm_g, k_norm_g, w_b_out, g_mem, w_mem_kv,
           mq_norm_g, mk_norm_g, w_m_out, w_o, g_ffn, w_up, conv_f_w, w_down):
    B, T, D = x.shape
    tm_in = min(TM_IN, T)
    tm_out = min(TM_OUT, T)
    k_sel = min(TOPK_MAX, T // 4)
    assert T <= 256 * POS_SPLIT, "key positions must split into two bf16-exact parts"
    row = lambda g: g.reshape(1, -1).astype(F32)

    kmT, vm = _mem_kv(mem, row(g_mem), w_mem_kv.astype(BF16), row(mk_norm_g))

    scale = HEAD_DIM ** -0.5
    gk_pad = jnp.pad(row(k_norm_g), ((0, 0), (0, HEAD_PAD - HEAD_DIM)))
    gqT = jnp.broadcast_to((q_norm_g.astype(F32) * (scale * LOG2E))[:, None], (HEAD_DIM, tm_in))
    part, g1, qaugT, kaug, vT, qiT, kw, wiT = _in_proj(
        x, row(g_mix), _split_w_in(*w_in), conv_a_w, w_a_out.astype(BF16), gk_pad, gqT,
        kmT, vm, row(mq_norm_g), w_m_out.astype(BF16), tm_in)

    attnT = _dsa(kw, qiT, wiT, kaug, qaugT, vT, k_sel)

    return _out_ffn(x, part, g1, attnT, w_b_out.astype(BF16), w_o.astype(BF16), row(g_ffn),
                    w_up.astype(BF16), conv_f_w, w_down.astype(BF16), tm_out)


def kernel(x, mem, g_mix, w_in, conv_a_w, w_a_out, q_norm_g, k_norm_g, w_b_out, g_mem, w_mem_kv,
           mq_norm_g, mk_norm_g, w_m_out, w_o, g_ffn, w_up, conv_f_w, w_down):
    for l in range(g_mix.shape[0]):
        x = _layer(x, mem, g_mix[l], (w_in, l), conv_a_w[l], w_a_out[l], q_norm_g[l], k_norm_g[l],
                   w_b_out[l], g_mem[l], w_mem_kv[l], mq_norm_g[l], mk_norm_g[l], w_m_out[l],
                   w_o[l], g_ffn[l], w_up[l], conv_f_w[l], w_down[l])
    return x
```

```python
import functools

import jax
import jax.numpy as jnp
import numpy as np
from jax import lax
from jax.experimental import pallas as pl
from jax.experimental.pallas import tpu as pltpu

F32 = jnp.float32
BF16 = jnp.bfloat16

EPS = 1e-6
N_HEADS = 8
HEAD_DIM = 64
IDX_HEADS = 8
IDX_DIM = 32
MEM_HEADS = 4
MEM_HEAD_DIM = 128
TOPK_MAX = 256
N_BRANCH = 3

LANE = 128
SUBLANE = 8
HEAD_PAD = LANE
POS_SPLIT = 8.0
LOG2E = 1.4426950408889634
SLOPE_TERMS = 3
VMEM_LIMIT = 60 * 1024 * 1024

TM_IN = 512
W_COLS = 256
TM_OUT = 512
QB = 256
KT = 256
FF_CHUNK = 256
FF_SLOTS = 4
VROWS = HEAD_DIM + 16
INTERP_STEPS = 8
BISECT_STEPS = 5
INTERP_CLAMP = 0.2
NEG_BIG = -1e30


def _const_spec(shape):
    nd = len(shape)
    return pl.BlockSpec(shape, lambda *_: (0,) * nd, pipeline_mode=pl.Buffered(1))


def _causal_conv3(u, prev8, w3, buf_ref, slot):
    rows = u.shape[0]
    outs = []
    for c in range(u.shape[1] // LANE):
        cs = slice(c * LANE, (c + 1) * LANE)
        buf_ref[slot + c, 0:SUBLANE, :] = prev8[:, cs]
        buf_ref[slot + c, SUBLANE:SUBLANE + rows, :] = u[:, cs]
        u1 = buf_ref[slot + c, pl.ds(SUBLANE - 1, rows), :]
        u2 = buf_ref[slot + c, pl.ds(SUBLANE - 2, rows), :]
        outs.append(w3[0:1, cs] * u2 + w3[1:2, cs] * u1 + w3[2:3, cs] * u[:, cs])
    return jnp.concatenate(outs, axis=1)


def _bf16_terms(c):
    rest, out = np.float32(c), []
    for _ in range(SLOPE_TERMS):
        term = np.float32(np.asarray(rest, dtype=jnp.bfloat16))
        out.append(float(term))
        rest = np.float32(rest - term)
    return out


def _rms_rows(x, g):
    return x * lax.rsqrt(jnp.mean(x * x, axis=-1, keepdims=True) + EPS) * g


def _mem_kv_kernel(mem_ref, gmem_ref, wkv_ref, gmk_ref, kmT_ref, vm_ref):
    memn = _rms_rows(mem_ref[0], gmem_ref[...]).astype(BF16)
    width = MEM_HEADS * MEM_HEAD_DIM
    km = jnp.dot(memn, wkv_ref[:, 0:width], preferred_element_type=F32)
    vm = jnp.dot(memn, wkv_ref[:, width:2 * width], preferred_element_type=F32)
    heads = []
    for h in range(MEM_HEADS):
        kh = km[:, h * MEM_HEAD_DIM:(h + 1) * MEM_HEAD_DIM]
        heads.append(_rms_rows(kh, gmk_ref[...]))
    kmT_ref[0] = jnp.concatenate(heads, axis=1).T.astype(BF16)
    vm_ref[0] = vm.astype(BF16)


def _mem_kv(mem, g_mem, w_kv_bf, mk_g):
    B, M, D = mem.shape
    width = MEM_HEADS * MEM_HEAD_DIM
    return pl.pallas_call(
        _mem_kv_kernel,
        grid=(B,),
        in_specs=[
            pl.BlockSpec((1, M, D), lambda b: (b, 0, 0)),
            _const_spec((1, D)),
            _const_spec((D, 2 * width)),
            _const_spec((1, MEM_HEAD_DIM)),
        ],
        out_specs=[
            pl.BlockSpec((1, width, M), lambda b: (b, 0, 0)),
            pl.BlockSpec((1, M, width), lambda b: (b, 0, 0)),
        ],
        out_shape=[
            jax.ShapeDtypeStruct((B, width, M), BF16),
            jax.ShapeDtypeStruct((B, M, width), BF16),
        ],
        compiler_params=pltpu.CompilerParams(
            dimension_semantics=("arbitrary",), vmem_limit_bytes=VMEM_LIMIT),
        name="mem_kv",
    )(mem, g_mem, w_kv_bf, mk_g)


class _InCols:
    def __init__(self, d_model):
        sc = d_model // 2
        self.sc = sc
        self.bch = 0
        self.q = self.bch + 3 * sc
        self.k = self.q + N_HEADS * HEAD_DIM
        self.v = self.k + N_HEADS * HEAD_DIM
        self.qi = self.v + N_HEADS * HEAD_DIM
        self.main = self.qi + IDX_HEADS * IDX_DIM


def _w_piece_kernel(rows, w_ref, out_ref):
    w = w_ref[0]
    if rows < out_ref.shape[1]:
        w = jnp.concatenate([w, jnp.zeros((out_ref.shape[1] - rows, w.shape[1]), F32)], axis=0)
    out_ref[...] = w.T.astype(BF16)


def _w_piece(w_t, layer, col0, cols, out_cols):
    D = w_t.shape[2]
    step = min(cols, W_COLS)
    assert cols % step == 0 and col0 % SUBLANE == 0 and (cols == step or out_cols == cols)
    return pl.pallas_call(
        functools.partial(_w_piece_kernel, step),
        grid=(cols // step,),
        in_specs=[pl.BlockSpec((pl.Element(1), pl.Element(step), pl.Element(D)),
                               lambda c: (layer, pl.multiple_of(col0 + c * step, SUBLANE), 0))],
        out_specs=pl.BlockSpec((D, out_cols // (cols // step)), lambda c: (0, c)),
        out_shape=jax.ShapeDtypeStruct((D, out_cols), BF16),
        compiler_params=pltpu.CompilerParams(
            dimension_semantics=("arbitrary",), vmem_limit_bytes=VMEM_LIMIT),
        name="w_piece",
    )(w_t)


def _split_w_in(w_in_stack, layer):
    _, D, width = w_in_stack.shape
    cols = _InCols(D)
    kw_cols = IDX_DIM + IDX_HEADS
    qm_cols = MEM_HEADS * MEM_HEAD_DIM
    assert cols.main + kw_cols + qm_cols + N_BRANCH * D == width
    w_t = jnp.swapaxes(w_in_stack, 1, 2)
    return (_w_piece(w_t, layer, 0, cols.main, cols.main),
            _w_piece(w_t, layer, cols.main, kw_cols, LANE),
            _w_piece(w_t, layer, cols.main + kw_cols, qm_cols, qm_cols),
            _w_piece(w_t, layer, cols.main + kw_cols + qm_cols, N_BRANCH * D, N_BRANCH * D))


def _in_proj_kernel(cols, tm, d_model,
                    x_ref, gmix_ref, w_ref, wkw_ref, wqm_ref, wg_ref, convw_ref, waout_ref, gk_ref, gqT_ref,
                    kmT_ref, vm_ref, gmq_ref, wmout_ref,
                    part_ref, g1_ref, qaugT_ref, kaug_ref, vT_ref, qiT_ref, kw_ref, wiT_ref,
                    halo_ref, cbuf_ref):
    j = pl.program_id(1)
    sc = cols.sc
    xn = _rms_rows(x_ref[0], gmix_ref[...]).astype(BF16)

    def proj(off, width, ref=w_ref):
        return jnp.dot(xn, ref[:, off:off + width], preferred_element_type=F32)

    @pl.when(j == 0)
    def _():
        halo_ref[...] = jnp.zeros_like(halo_ref)

    ch = proj(cols.bch + sc, sc) * proj(cols.bch + 2 * sc, sc)
    conv = _causal_conv3(ch, halo_ref[...], convw_ref[...], cbuf_ref, 0)
    halo_ref[...] = ch[tm - SUBLANE:tm, :]
    ua = (proj(cols.bch, sc) * conv).astype(BF16)
    ya = jnp.dot(ua, waout_ref[...], preferred_element_type=F32)

    qT = proj(cols.q, N_HEADS * HEAD_DIM).T
    row8 = lax.broadcasted_iota(jnp.int32, (SUBLANE, tm), 0)
    zpad = jnp.zeros((HEAD_PAD - HEAD_DIM - SUBLANE, tm), F32)
    for h in range(N_HEADS):
        blk = qT[h * HEAD_DIM:(h + 1) * HEAD_DIM, :]
        r = lax.rsqrt(jnp.sum(blk * blk, axis=0, keepdims=True) * (1.0 / HEAD_DIM) + EPS)
        aug = jnp.zeros((SUBLANE, tm), F32)
        for i, term in enumerate(_bf16_terms(2.0 ** (-8.0 * (h + 1) / N_HEADS) * LOG2E)):
            aug = jnp.where((row8 == i) | (row8 == i + SLOPE_TERMS), term, aug)
        full = jnp.concatenate([blk * r * gqT_ref[...], aug, zpad], axis=0)
        qaugT_ref[0, h * HEAD_PAD:(h + 1) * HEAD_PAD, :] = full.astype(BF16)

    kraw = proj(cols.k, N_HEADS * HEAD_DIM)
    posf = (j * tm + lax.broadcasted_iota(jnp.int32, (tm, HEAD_PAD), 0)).astype(F32)
    lane = lax.broadcasted_iota(jnp.int32, (tm, HEAD_PAD), 1)
    pos_hi = jnp.floor(posf * (1.0 / POS_SPLIT)) * POS_SPLIT
    in_hi = (lane >= HEAD_DIM) & (lane < HEAD_DIM + SLOPE_TERMS)
    in_lo = (lane >= HEAD_DIM + SLOPE_TERMS) & (lane < HEAD_DIM + 2 * SLOPE_TERMS)
    posmat = jnp.where(in_hi, pos_hi, jnp.where(in_lo, posf - pos_hi, 0.0))
    for h in range(N_HEADS):
        pair = kraw[:, (h // 2) * LANE:(h // 2 + 1) * LANE]
        if h % 2:
            pair = pltpu.roll(pair, HEAD_DIM, axis=1)
        kg = jnp.where(lane < HEAD_DIM, pair, 0.0)
        r = lax.rsqrt(jnp.sum(kg * kg, axis=-1, keepdims=True) * (1.0 / HEAD_DIM) + EPS)
        kaug_ref[0, :, h * HEAD_PAD:(h + 1) * HEAD_PAD] = (kg * r * gk_ref[...] + posmat).astype(BF16)

    vT_ref[0] = proj(cols.v, N_HEADS * HEAD_DIM).T.astype(BF16)
    qiT_ref[0] = proj(cols.qi, IDX_HEADS * IDX_DIM).T.astype(BF16)
    qm = proj(0, MEM_HEADS * MEM_HEAD_DIM, wqm_ref)
    mem_scale = MEM_HEAD_DIM ** -0.5
    heads = []
    for h in range(MEM_HEADS):
        sl = slice(h * MEM_HEAD_DIM, (h + 1) * MEM_HEAD_DIM)
        qh = _rms_rows(qm[:, sl], gmq_ref[...]).astype(BF16)
        lg = jnp.dot(qh, kmT_ref[0, sl, :], preferred_element_type=F32) * mem_scale
        p = jnp.exp(lg - jnp.max(lg, axis=-1, keepdims=True))
        denom = jnp.sum(p, axis=-1, keepdims=True)
        oh = jnp.dot(p.astype(BF16), vm_ref[0, :, sl], preferred_element_type=F32)
        heads.append(oh / denom)
    om = jnp.concatenate(heads, axis=1).astype(BF16)
    ym = jnp.dot(om, wmout_ref[...], preferred_element_type=F32)

    g1_ref[0] = jax.nn.sigmoid(proj(d_model, d_model, wg_ref)).astype(BF16)
    g0 = jax.nn.sigmoid(proj(0, d_model, wg_ref))
    g2 = jax.nn.sigmoid(proj(2 * d_model, d_model, wg_ref))
    part_ref[0] = (g0 * ya + g2 * ym).astype(BF16)

    kw = proj(0, LANE, wkw_ref)
    kw_ref[0] = kw.astype(BF16)
    wiT_ref[0] = kw.T[IDX_DIM:IDX_DIM + IDX_HEADS, :]


def _in_proj(x, g_mix, w_pieces, conv_a_w, w_a_out_bf, gk_pad, gqT, kmT, vm, mq_g, w_m_out_bf, tm):
    B, T, D = x.shape
    cols = _InCols(D)
    M = vm.shape[1]
    memw = MEM_HEADS * MEM_HEAD_DIM
    tok = lambda w: pl.BlockSpec((1, tm, w), lambda b, j: (b, j, 0))
    tokT = lambda w: pl.BlockSpec((1, w, tm), lambda b, j: (b, 0, j))
    return pl.pallas_call(
        functools.partial(_in_proj_kernel, cols, tm, D),
        grid=(B, T // tm),
        in_specs=[
            tok(D),
            _const_spec((1, D)),
            _const_spec((D, cols.main)),
            _const_spec((D, LANE)),
            _const_spec((D, memw)),
            _const_spec((D, N_BRANCH * D)),
            _const_spec((3, cols.sc)),
            _const_spec((cols.sc, D)),
            _const_spec((1, HEAD_PAD)),
            _const_spec((HEAD_DIM, tm)),
            pl.BlockSpec((1, memw, M), lambda b, j: (b, 0, 0)),
            pl.BlockSpec((1, M, memw), lambda b, j: (b, 0, 0)),
            _const_spec((1, MEM_HEAD_DIM)),
            _const_spec((memw, D)),
        ],
        out_specs=[
            tok(D), tok(D),
            tokT(N_HEADS * HEAD_PAD), tok(N_HEADS * HEAD_PAD),
            tokT(N_HEADS * HEAD_DIM), tokT(IDX_HEADS * IDX_DIM),
            tok(LANE), tokT(IDX_HEADS),
        ],
        out_shape=[
            jax.ShapeDtypeStruct((B, T, D), BF16),
            jax.ShapeDtypeStruct((B, T, D), BF16),
            jax.ShapeDtypeStruct((B, N_HEADS * HEAD_PAD, T), BF16),
            jax.ShapeDtypeStruct((B, T, N_HEADS * HEAD_PAD), BF16),
            jax.ShapeDtypeStruct((B, N_HEADS * HEAD_DIM, T), BF16),
            jax.ShapeDtypeStruct((B, IDX_HEADS * IDX_DIM, T), BF16),
            jax.ShapeDtypeStruct((B, T, LANE), BF16),
            jax.ShapeDtypeStruct((B, IDX_HEADS, T), F32),
        ],
        scratch_shapes=[pltpu.VMEM((SUBLANE, cols.sc), F32),
                        pltpu.VMEM((cols.sc // LANE, tm + SUBLANE, LANE), F32)],
        compiler_params=pltpu.CompilerParams(
            dimension_semantics=("arbitrary", "arbitrary"), vmem_limit_bytes=VMEM_LIMIT),
        name="in_proj",
    )(x, g_mix, *w_pieces, conv_a_w, w_a_out_bf, gk_pad, gqT, kmT, vm, mq_g, w_m_out_bf)


def _dsa_kernel(k_sel, kw_ref, qiT_ref, wiT_ref, kaug_ref, qaugT_ref, vT_ref, out_ref,
                s_ref, lg_ref, acc_ref):
    i = pl.program_id(1)
    n_tiles = i + 1
    qpos = i * QB + lax.broadcasted_iota(jnp.int32, (1, QB), 1)
    row = lax.broadcasted_iota(jnp.int32, (KT, QB), 0)
    groups = KT // SUBLANE
    inf = jnp.float32(jnp.inf)

    def tile_start(t):
        return pl.multiple_of(t * KT, KT)

    def for_tiles(body, init, group=2):
        def run(t0, count, carry):
            for j in range(count):
                carry = body(t0 + j, carry)
            return carry
        carry = lax.fori_loop(0, n_tiles // group, lambda p, c: run(group * p, group, c), init)
        done = (n_tiles // group) * group
        part = group // 2
        while part >= 1:
            take = (n_tiles & part) != 0
            carry = lax.cond(take, lambda c, d=done, n=part: run(d, n, c), lambda c: c, carry)
            done = done + jnp.where(take, part, 0)
            part //= 2
        return carry

    zrows = jnp.zeros((LANE - IDX_DIM, QB), BF16)

    def score_tile(t, carry):
        lo_part, hi_part = carry
        r0 = tile_start(t)
        kw_t = kw_ref[0, pl.ds(r0, KT), :]
        acc = jnp.zeros((KT, QB), F32)
        for h in range(IDX_HEADS):
            rhs = jnp.concatenate([qiT_ref[0, h * IDX_DIM:(h + 1) * IDX_DIM, :], zrows], axis=0)
            lg = jnp.dot(kw_t, rhs, preferred_element_type=F32)
            acc = acc + wiT_ref[0, h:h + 1, :] * jnp.maximum(lg, 0.0)
        valid = r0 + row <= qpos
        s_ref[pl.ds(r0, KT), :] = jnp.where(valid, acc, -inf)
        lo_t = jnp.min(jnp.where(valid, acc, inf).reshape(groups, SUBLANE, QB), axis=0)
        hi_t = jnp.max(jnp.where(valid, acc, -inf).reshape(groups, SUBLANE, QB), axis=0)
        return jnp.minimum(lo_part, lo_t), jnp.maximum(hi_part, hi_t)

    lo_part, hi_part = for_tiles(
        score_tile,
        (jnp.full((SUBLANE, QB), jnp.inf, F32), jnp.full((SUBLANE, QB), -jnp.inf, F32)), group=4)

    def tree(x3, op):
        while x3.shape[0] > 1:
            half = x3.shape[0] // 2
            x3 = op(x3[:half], x3[half:])
        return x3[0]

    def load3(t):
        return s_ref[pl.ds(tile_start(t), KT), :].reshape(groups, SUBLANE, QB)

    def bcast8(v):
        return jnp.broadcast_to(v, (SUBLANE, QB))[None]

    def count_ge(v):
        v8 = bcast8(v)

        def body(t, part):
            return part + tree(jnp.where(load3(t) >= v8, 1.0, 0.0), jnp.add)

        part = for_tiles(body, jnp.zeros((SUBLANE, QB), F32))
        return jnp.sum(part, axis=0, keepdims=True)

    def bracket_ends(lo, hi):
        lo8, hi8 = bcast8(lo), bcast8(hi)

        def body(t, carry):
            s3 = load3(t)
            a_t = tree(jnp.where(s3 >= lo8, s3, inf), jnp.minimum)
            b_t = tree(jnp.where(s3 < hi8, s3, -inf), jnp.maximum)
            return jnp.minimum(carry[0], a_t), jnp.maximum(carry[1], b_t)

        a_part, b_part = for_tiles(
            body, (jnp.full((SUBLANE, QB), jnp.inf, F32), jnp.full((SUBLANE, QB), -jnp.inf, F32)))
        return jnp.min(a_part, axis=0, keepdims=True), jnp.max(b_part, axis=0, keepdims=True)

    kf = jnp.minimum(qpos + 1, k_sel).astype(F32)
    top = jnp.max(hi_part, axis=0, keepdims=True)
    lo0 = jnp.min(lo_part, axis=0, keepdims=True)
    hi0 = top + jnp.maximum(jnp.abs(top), 1.0) * 1e-3
    clo0 = (qpos + 1).astype(F32)

    def split(st, mid):
        lo, hi, clo, chi = st
        c = count_ge(mid)
        ge = c >= kf
        return (jnp.where(ge, mid, lo), jnp.where(ge, hi, mid),
                jnp.where(ge, c, clo), jnp.where(ge, chi, c))

    def interpolate(_, st):
        lo, hi, clo, chi = st
        frac = jnp.clip((clo - kf + 0.5) / jnp.maximum(clo - chi, 1.0), INTERP_CLAMP, 1.0 - INTERP_CLAMP)
        mid = lo + (hi - lo) * frac
        return split(st, jnp.where(mid <= lo, hi, mid))

    def bisect(_, st):
        lo, hi = st[0], st[1]
        mid = lo + (hi - lo) * 0.5
        return split(st, jnp.where(mid <= lo, hi, mid))

    st = lax.fori_loop(0, INTERP_STEPS, interpolate, (lo0, hi0, clo0, jnp.zeros((1, QB), F32)))
    st = lax.fori_loop(0, BISECT_STEPS, bisect, st)

    def unresolved(a, b):
        return jnp.max(jnp.where(a < b, 1.0, 0.0))

    def refine(carry):
        (lo, hi, clo, chi), a, b = carry[:4], carry[4], carry[5]
        mid = a + (b - a) * 0.5
        mid = jnp.where(mid <= a, b, mid)
        mid8 = bcast8(mid)

        def body(t, parts):
            s3 = load3(t)
            ge = s3 >= mid8
            return (parts[0] + tree(jnp.where(ge, 1.0, 0.0), jnp.add),
                    jnp.minimum(parts[1], tree(jnp.where(ge, s3, inf), jnp.minimum)),
                    jnp.maximum(parts[2], tree(jnp.where(ge, -inf, s3), jnp.maximum)))

        parts = for_tiles(body, (jnp.zeros((SUBLANE, QB), F32), jnp.full((SUBLANE, QB), jnp.inf, F32),
                                 jnp.full((SUBLANE, QB), -jnp.inf, F32)))
        c = jnp.sum(parts[0], axis=0, keepdims=True)
        above = jnp.min(parts[1], axis=0, keepdims=True)
        below = jnp.max(parts[2], axis=0, keepdims=True)
        ge = c >= kf
        a, b = jnp.where(ge, above, a), jnp.where(ge, b, below)
        return (jnp.where(ge, mid, lo), jnp.where(ge, hi, mid), jnp.where(ge, c, clo), jnp.where(ge, chi, c),
                a, b, unresolved(a, b))

    a0, b0 = bracket_ends(st[0], st[1])
    lo, hi, clo, cgt, tau, _, _ = lax.while_loop(
        lambda c: c[6] > 0.0, refine, st + (a0, b0, unresolved(a0, b0)))

    need = kf - cgt

    @pl.when(jnp.max(jnp.where(clo > kf, 1.0, 0.0)) > 0.0)
    def _():
        tri = jnp.where(row >= lax.broadcasted_iota(jnp.int32, (KT, KT), 1), 1.0, 0.0).astype(BF16)

        def drop_tile(t, before):
            r0 = tile_start(t)
            s_t = s_ref[pl.ds(r0, KT), :]
            tied = jnp.where(s_t == tau, 1.0, 0.0)
            rank = before + jnp.dot(tri, tied.astype(BF16), preferred_element_type=F32)
            s_ref[pl.ds(r0, KT), :] = jnp.where(tied * rank > need, -inf, s_t)
            return before + jnp.sum(tied, axis=0, keepdims=True)

        for_tiles(drop_tile, jnp.zeros((1, QB), F32))

    def logits_tile(t, mparts):
        r0 = tile_start(t)
        bias = jnp.where(s_ref[pl.ds(r0, KT), :] >= tau, 0.0, -inf)
        out = []
        for h in range(N_HEADS):
            hs = slice(h * HEAD_PAD, (h + 1) * HEAD_PAD)
            s = jnp.dot(kaug_ref[0, pl.ds(r0, KT), hs], qaugT_ref[0, hs, :],
                        preferred_element_type=F32) + bias
            lg_ref[h, pl.ds(r0, KT), :] = s
            out.append(jnp.maximum(mparts[h], jnp.max(s.reshape(groups, SUBLANE, QB), axis=0)))
        return tuple(out)

    mparts = for_tiles(logits_tile,
                       tuple(jnp.full((SUBLANE, QB), NEG_BIG, F32) for _ in range(N_HEADS)), group=4)
    m_rows = [jnp.max(mp, axis=0, keepdims=True) for mp in mparts]

    acc_ref[...] = jnp.zeros(acc_ref.shape, F32)

    ones_rows = jnp.ones((2 * SUBLANE, KT), BF16)

    def pv_tile(t, carry):
        r0 = tile_start(t)
        for h in range(N_HEADS):
            vs = slice(h * HEAD_DIM, (h + 1) * HEAD_DIM)
            ws = slice(h * VROWS, (h + 1) * VROWS)
            p = jnp.exp2((lg_ref[h, pl.ds(r0, KT), :] - m_rows[h]).astype(BF16))
            v_ones = jnp.concatenate([vT_ref[0, vs, pl.ds(r0, KT)], ones_rows], axis=0)
            acc_ref[ws, :] += jnp.dot(v_ones, p, preferred_element_type=F32)
        return carry

    for_tiles(pv_tile, 0, group=4)

    for h in range(N_HEADS):
        vs = slice(h * HEAD_DIM, (h + 1) * HEAD_DIM)
        denom = acc_ref[h * VROWS + HEAD_DIM:h * VROWS + HEAD_DIM + 1, :]
        out_ref[0, vs, :] = (acc_ref[h * VROWS:h * VROWS + HEAD_DIM, :] / denom).astype(BF16)


def _dsa(kw, qiT, wiT, kaug, qaugT, vT, k_sel):
    B, T, _ = kaug.shape
    assert T % QB == 0 and QB == KT
    att = N_HEADS * HEAD_DIM
    qblk = lambda w: pl.BlockSpec((1, w, QB), lambda b, i: (b, 0, i))
    return pl.pallas_call(
        functools.partial(_dsa_kernel, k_sel),
        grid=(B, T // QB),
        in_specs=[
            pl.BlockSpec((1, T, LANE), lambda b, i: (b, 0, 0)),
            qblk(IDX_HEADS * IDX_DIM),
            qblk(IDX_HEADS),
            pl.BlockSpec((1, T, N_HEADS * HEAD_PAD), lambda b, i: (b, 0, 0)),
            qblk(N_HEADS * HEAD_PAD),
            pl.BlockSpec((1, att, T), lambda b, i: (b, 0, 0)),
        ],
        out_specs=qblk(att),
        out_shape=jax.ShapeDtypeStruct((B, att, T), BF16),
        scratch_shapes=[
            pltpu.VMEM((T, QB), F32),
            pltpu.VMEM((N_HEADS, T, QB), F32),
            pltpu.VMEM((N_HEADS * VROWS, QB), F32),
        ],
        compiler_params=pltpu.CompilerParams(
            dimension_semantics=("arbitrary", "arbitrary"), vmem_limit_bytes=VMEM_LIMIT),
        name="dsa",
    )(kw, qiT, wiT, kaug, qaugT, vT)


def _out_ffn_kernel(tm, d_ff, x_ref, part_ref, g1_ref, attnT_ref, wb_ref, wo_ref, gffn_ref,
                    wup_ref, convf_ref, wdown_ref, out_ref, halo_ref, cbuf_ref, hmid_ref):
    j = pl.program_id(1)
    yb = lax.dot_general(attnT_ref[0], wb_ref[...], (((0,), (0,)), ((), ())),
                         preferred_element_type=F32)
    merged = (part_ref[0].astype(F32) + g1_ref[0].astype(F32) * yb).astype(BF16)
    x1 = x_ref[0] + jnp.dot(merged, wo_ref[...], preferred_element_type=F32)
    xn2 = _rms_rows(x1, gffn_ref[...]).astype(BF16)

    @pl.when(j == 0)
    def _():
        halo_ref[...] = jnp.zeros_like(halo_ref)

    def up_proj(c):
        return tuple(jnp.dot(xn2, wup_ref[:, off:off + FF_CHUNK], preferred_element_type=F32)
                     for off in (c * FF_CHUNK, d_ff + c * FF_CHUNK))

    def conv_chunk(up, off, slot):
        cs = slice(off, off + FF_CHUNK)
        out = _causal_conv3(up, halo_ref[:, cs], convf_ref[:, cs], cbuf_ref, slot)
        halo_ref[:, cs] = up[tm - SUBLANE:tm, :]
        return out

    n_chunks = d_ff // FF_CHUNK
    blocks = FF_CHUNK // LANE
    ups = up_proj(0)
    for c in range(n_chunks):
        nxt = up_proj(c + 1) if c + 1 < n_chunks else None
        slot = (c % FF_SLOTS) * 2 * blocks
        gate = conv_chunk(ups[0], c * FF_CHUNK, slot)
        val = conv_chunk(ups[1], d_ff + c * FF_CHUNK, slot + blocks)
        hmid_ref[:, c * FF_CHUNK:(c + 1) * FF_CHUNK] = (gate * jax.nn.sigmoid(gate) * val).astype(BF16)
        ups = nxt
    out_ref[0] = x1 + jnp.dot(hmid_ref[...], wdown_ref[...], preferred_element_type=F32)


def _out_ffn(x, part, g1, attnT, w_b_bf, w_o_bf, g_ffn, w_up_bf, conv_f_w, w_down_bf, tm):
    B, T, D = x.shape
    d_ff = w_down_bf.shape[0]
    att = N_HEADS * HEAD_DIM
    assert d_ff % FF_CHUNK == 0
    tok = lambda w: pl.BlockSpec((1, tm, w), lambda b, j: (b, j, 0))
    return pl.pallas_call(
        functools.partial(_out_ffn_kernel, tm, d_ff),
        grid=(B, T // tm),
        in_specs=[
            tok(D), tok(D), tok(D),
            pl.BlockSpec((1, att, tm), lambda b, j: (b, 0, j)),
            _const_spec((att, D)),
            _const_spec((D, D)),
            _const_spec((1, D)),
            _const_spec((D, 2 * d_ff)),
            _const_spec((3, 2 * d_ff)),
            _const_spec((d_ff, D)),
        ],
        out_specs=tok(D),
        out_shape=jax.ShapeDtypeStruct((B, T, D), F32),
        scratch_shapes=[pltpu.VMEM((SUBLANE, 2 * d_ff), F32),
                        pltpu.VMEM((FF_SLOTS * 2 * FF_CHUNK // LANE, tm + SUBLANE, LANE), F32),
                        pltpu.VMEM((tm, d_ff), BF16)],
        compiler_params=pltpu.CompilerParams(
            dimension_semantics=("arbitrary", "arbitrary"), vmem_limit_bytes=VMEM_LIMIT),
        name="out_ffn",
    )(x, part, g1, attnT, w_b_bf, w_o_bf, g_ffn, w_up_bf, conv_f_w, w_down_bf)


def _layer(x, mem, g_mix, w_in, conv_a_w, w_a_out, q_norm_g, k_norm_g, w_b_out, g_mem, w_mem_kv,
           mq_norm_g, mk_norm_g, w_m_out, w_o, g_ffn, w_up, conv_f_w, w_down):
    B, T, D = x.shape
    tm_in = min(TM_IN, T)
    tm_out = min(TM_OUT, T)
    k_sel = min(TOPK_MAX, T // 4)
    assert T <= 256 * POS_SPLIT, "key positions must split into two bf16-exact parts"
    row = lambda g: g.reshape(1, -1).astype(F32)

    kmT, vm = _mem_kv(mem, row(g_mem), w_mem_kv.astype(BF16), row(mk_norm_g))

    scale = HEAD_DIM ** -0.5
    gk_pad = jnp.pad(row(k_norm_g), ((0, 0), (0, HEAD_PAD - HEAD_DIM)))
    gqT = jnp.broadcast_to((q_norm_g.astype(F32) * (scale * LOG2E))[:, None], (HEAD_DIM, tm_in))
    part, g1, qaugT, kaug, vT, qiT, kw, wiT = _in_proj(
        x, row(g_mix), _split_w_in(*w_in), conv_a_w, w_a_out.astype(BF16), gk_pad, gqT,
        kmT, vm, row(mq_norm_g), w_m_out.astype(BF16), tm_in)

    attnT = _dsa(kw, qiT, wiT, kaug, qaugT, vT, k_sel)

    return _out_ffn(x, part, g1, attnT, w_b_out.astype(BF16), w_o.astype(BF16), row(g_ffn),
                    w_up.astype(BF16), conv_f_w, w_down.astype(BF16), tm_out)


def kernel(x, mem, g_mix, w_in, conv_a_w, w_a_out, q_norm_g, k_norm_g, w_b_out, g_mem, w_mem_kv,
           mq_norm_g, mk_norm_g, w_m_out, w_o, g_ffn, w_up, conv_f_w, w_down):
    for l in range(g_mix.shape[0]):
        x = _layer(x, mem, g_mix[l], (w_in, l), conv_a_w[l], w_a_out[l], q_norm_g[l], k_norm_g[l],
                   w_b_out[l], g_mem[l], w_mem_kv[l], mq_norm_g[l], mk_norm_g[l], w_m_out[l],
                   w_o[l], g_ffn[l], w_up[l], conv_f_w[l], w_down[l])
    return x
```
